```python
import math
import jax, jax.numpy as jnp
from jax import lax
import numpy as np

D_MODEL = 1024
BATCH = 16
SEQ = 2048
DEPTH = 1
DEC_BATCH = 32
DEC_SEQ = 4
PAST_LEN = 16384
PAGE_SIZE = 128

HG_HEADS = 4
HG_DK = 128
HG_DV = 128
HG_WIDTH_K = HG_HEADS * HG_DK
HG_WIDTH_V = HG_HEADS * HG_DV
HG_CHUNK = 64
AT_HEADS = 8
AT_DH = 64
AT_WIDTH = AT_HEADS * AT_DH
MOBA_BLOCK = 256
MOBA_TOPK = 3
Q_BLOCK = 128
ROPE_THETA = 10000.0
D_FF = -(-8 * D_MODEL // (3 * 256)) * 256
EPS = 1e-6
IN_SIZES = (HG_WIDTH_K, HG_WIDTH_K, HG_WIDTH_V, HG_WIDTH_V, AT_WIDTH, AT_WIDTH, AT_WIDTH, D_MODEL, D_MODEL)
IN_COLS = sum(IN_SIZES)

kernel_name = "hybrid_hgrn2_moba_decoder_step"


def rmsnorm(x, w):
    xf = x.astype(jnp.float32)
    y = xf * lax.rsqrt(jnp.mean(xf * xf, axis=-1, keepdims=True) + EPS)
    return (y * w.astype(jnp.float32)).astype(x.dtype)


def rope(x, pos):
    half = x.shape[-1] // 2
    inv = ROPE_THETA ** (-jnp.arange(half, dtype=jnp.float32) / half)
    ang = pos.astype(jnp.float32)[:, None] * inv[None, :]
    cos = jnp.cos(ang)[None, :, None, :]
    sin = jnp.sin(ang)[None, :, None, :]
    xf = x.astype(jnp.float32)
    x1, x2 = xf[..., :half], xf[..., half:]
    return jnp.concatenate([x1 * cos - x2 * sin, x2 * cos + x1 * sin], axis=-1).astype(x.dtype)


def hgrn2_recurrence(q, k, v, logf, s0, chunk):
    B, T, H, _ = q.shape
    n = T // chunk

    def to_chunks(a):
        return a.astype(jnp.float32).reshape(B, n, chunk, H, a.shape[-1]).transpose(1, 0, 3, 2, 4)

    qc, kc, vc, gc = to_chunks(q), to_chunks(k), to_chunks(v), to_chunks(logf)
    causal = jnp.tril(jnp.ones((chunk, chunk), dtype=bool))[None, None, :, :, None]

    def step(S, inp):
        qi, ki, vi, gi = inp
        b = jnp.cumsum(gi, axis=2)
        o_inter = jnp.einsum('bhtd,bhde->bhte', qi * jnp.exp(b), S)
        diff = b[:, :, :, None, :] - b[:, :, None, :, :]
        decay = jnp.exp(jnp.where(causal, diff, -jnp.inf))
        a = jnp.einsum('bhtd,bhsd,bhtsd->bhts', qi, ki, decay)
        o = o_inter + jnp.einsum('bhts,bhse->bhte', a, vi)
        b_last = b[:, :, -1, :]
        S_new = jnp.exp(b_last)[..., None] * S + jnp.einsum('bhsd,bhse->bhde', ki * jnp.exp(b_last[:, :, None, :] - b), vi)
        return S_new, o

    S, o = lax.scan(step, s0.astype(jnp.float32), (qc, kc, vc, gc))
    o = o.transpose(1, 0, 3, 2, 4).reshape(B, T, H, v.shape[-1])
    return o, S


def moba_attend(q, k, v, q_pos):
    B, L, H, dh = k.shape
    nb = -(-L // MOBA_BLOCK)
    pad = nb * MOBA_BLOCK - L

    def blocks(a):
        if pad:
            a = jnp.pad(a, ((0, 0), (0, pad), (0, 0), (0, 0)))
        return a.reshape(B, nb, MOBA_BLOCK, H, dh)

    kb, vb = blocks(k), blocks(v)
    kmean = jnp.mean(kb.astype(jnp.float32), axis=2)
    ksel = max(1, min(MOBA_TOPK, nb - 1))
    scale = dh ** -0.5
    gather_sel = jax.vmap(jax.vmap(lambda a_bh, s_bh: a_bh[s_bh], in_axes=(2, 0)), in_axes=(0, 0))

    def chunk_fn(args):
        qc, pc = args
        Qc = qc.shape[1]
        qh = qc.astype(jnp.float32).transpose(0, 2, 1, 3) * scale
        own = pc // MOBA_BLOCK
        gate = jnp.einsum('bhqd,bnhd->bhqn', qh, kmean)
        past_ok = jnp.arange(nb)[None, :] < own[:, None]
        gate = jnp.where(past_ok[None, None], gate, -jnp.inf)
        _, sel = lax.top_k(gate, ksel)
        sel_ok = sel < own[None, None, :, None]
        k_sel = gather_sel(kb, sel).astype(jnp.float32)
        v_sel = gather_sel(vb, sel).astype(jnp.float32)
        k_own = kb[:, own].astype(jnp.float32)
        v_own = vb[:, own].astype(jnp.float32)
        s_sel = jnp.einsum('bhqd,bhqkld->bhqkl', qh, k_sel)
        s_sel = jnp.where(sel_ok[..., None], s_sel, -jnp.inf).reshape(B, H, Qc, ksel * MOBA_BLOCK)
        kpos_own = own[:, None] * MOBA_BLOCK + jnp.arange(MOBA_BLOCK)[None, :]
        s_own = jnp.einsum('bhqd,bqlhd->bhql', qh, k_own)
        s_own = jnp.where((kpos_own <= pc[:, None])[None, None], s_own, -jnp.inf)
        p = jax.nn.softmax(jnp.concatenate([s_sel, s_own], axis=-1), axis=-1)
        p_sel = p[..., :ksel * MOBA_BLOCK].reshape(B, H, Qc, ksel, MOBA_BLOCK)
        p_own = p[..., ksel * MOBA_BLOCK:]
        o = jnp.einsum('bhqkl,bhqkld->bhqd', p_sel, v_sel) + jnp.einsum('bhql,bqlhd->bhqd', p_own, v_own)
        return o.transpose(0, 2, 1, 3).astype(qc.dtype)

    Tq = q.shape[1]
    if Tq > Q_BLOCK and Tq % Q_BLOCK == 0:
        nq = Tq // Q_BLOCK
        qs = q.reshape(B, nq, Q_BLOCK, H, dh).transpose(1, 0, 2, 3, 4)
        ps = q_pos.reshape(nq, Q_BLOCK)
        o = lax.map(chunk_fn, (qs, ps))
        return o.transpose(1, 0, 2, 3, 4).reshape(B, Tq, H, dh)
    return chunk_fn((q, q_pos))


def hybrid_layer(x, c, pos0, s0, k_past, v_past, lb, w_ada, b_ada, g_pre1, g_post1, w_in, g_onorm,
                 w_proj_a, w_proj_b, w_out, g_pre2, g_post2, w_gu, w_down):
    B, T, _ = x.shape
    mod = jnp.dot(jax.nn.silu(c), w_ada) + b_ada
    sh1, sc1, gt1, sh2, sc2, gt2 = [m[:, None, :] for m in jnp.split(mod, 6, axis=-1)]
    h = rmsnorm(x, g_pre1) * (1.0 + sc1) + sh1
    proj = jnp.dot(h, w_in)
    splits = np.cumsum(IN_SIZES)[:-1].tolist()
    hq, hf, hi, hg, aq, ak, av, ua, ub = jnp.split(proj, splits, axis=-1)
    f = lb + (1.0 - lb) * jax.nn.sigmoid(hf.astype(jnp.float32))
    logf = jnp.log(f)
    kin = 1.0 - f
    chunk = HG_CHUNK if T % HG_CHUNK == 0 else T
    o_a, s_new = hgrn2_recurrence(hq.reshape(B, T, HG_HEADS, HG_DK), kin.reshape(B, T, HG_HEADS, HG_DK),
                                  hi.reshape(B, T, HG_HEADS, HG_DV), logf.reshape(B, T, HG_HEADS, HG_DK), s0, chunk)
    o_a = rmsnorm(o_a, g_onorm) * jax.nn.silu(hg.reshape(B, T, HG_HEADS, HG_DV).astype(jnp.float32))
    o_a = o_a.reshape(B, T, HG_WIDTH_V).astype(x.dtype)
    pos = pos0 + jnp.arange(T, dtype=jnp.int32)
    q = rope(aq.reshape(B, T, AT_HEADS, AT_DH), pos)
    k = rope(ak.reshape(B, T, AT_HEADS, AT_DH), pos)
    v = av.reshape(B, T, AT_HEADS, AT_DH)
    k_all = jnp.concatenate([k_past.astype(k.dtype), k], axis=1)
    v_all = jnp.concatenate([v_past.astype(v.dtype), v], axis=1)
    o_b = moba_attend(q, k_all, v_all, pos).reshape(B, T, AT_WIDTH)
    merged = jax.nn.sigmoid(ua) * jnp.dot(o_a, w_proj_a) + jax.nn.sigmoid(ub) * jnp.dot(o_b, w_proj_b)
    x = x + gt1 * rmsnorm(jnp.dot(merged, w_out), g_post1)
    h = rmsnorm(x, g_pre2) * (1.0 + sc2) + sh2
    g_, u_ = jnp.split(jnp.dot(h, w_gu), 2, axis=-1)
    x = x + gt2 * rmsnorm(jnp.dot(jax.nn.silu(g_) * u_, w_down), g_post2)
    return x, s_new, k, v


def setup_inputs(seed: int = 0) -> dict:
    key = jax.random.key(seed)
    ks = jax.random.split(key, 24)
    f32 = jnp.float32
    n_pages = PAST_LEN // PAGE_SIZE
    n_pool = (5 * DEC_BATCH * n_pages + 3) // 4

    def nrm(k, shape, scale):
        return jax.random.normal(k, shape, f32) * scale

    def gain(k, shape):
        return 1.0 + 0.02 * jax.random.normal(k, shape, f32)

    page_table = jax.random.permutation(ks[7], n_pool)[:DEC_BATCH * n_pages].reshape(DEC_BATCH, n_pages).astype(jnp.int32)
    return {
        "x_prompt": nrm(ks[0], (BATCH, SEQ, D_MODEL), 1.0),
        "x_sample": nrm(ks[1], (DEC_BATCH, DEC_SEQ, D_MODEL), 1.0),
        "c_prompt": nrm(ks[2], (BATCH, D_MODEL), 1.0),
        "c_sample": nrm(ks[3], (DEC_BATCH, D_MODEL), 1.0),
        "state_hgrn": nrm(ks[4], (DEPTH, DEC_BATCH, HG_HEADS, HG_DK, HG_DV), 0.5),
        "cache_k": nrm(ks[5], (DEPTH, n_pool, PAGE_SIZE, AT_HEADS, AT_DH), 1.0),
        "cache_v": nrm(ks[6], (DEPTH, n_pool, PAGE_SIZE, AT_HEADS, AT_DH), 1.0),
        "page_table": page_table,
        "hg_lb": nrm(ks[8], (DEPTH + 1, HG_WIDTH_K), 1.0),
        "w_ada": nrm(ks[9], (DEPTH, D_MODEL, 6 * D_MODEL), 0.5 * D_MODEL ** -0.5),
        "b_ada": nrm(ks[10], (DEPTH, 6 * D_MODEL), 0.01),
        "g_pre1": gain(ks[11], (DEPTH, D_MODEL)),
        "g_post1": gain(ks[12], (DEPTH, D_MODEL)),
        "w_in": nrm(ks[13], (DEPTH, D_MODEL, IN_COLS), D_MODEL ** -0.5),
        "g_onorm": gain(ks[14], (DEPTH, HG_DV)),
        "w_proj_a": nrm(ks[15], (DEPTH, HG_WIDTH_V, D_MODEL), HG_WIDTH_V ** -0.5),
        "w_proj_b": nrm(ks[16], (DEPTH, AT_WIDTH, D_MODEL), AT_WIDTH ** -0.5),
        "w_out": nrm(ks[17], (DEPTH, D_MODEL, D_MODEL), D_MODEL ** -0.5),
        "g_pre2": gain(ks[18], (DEPTH, D_MODEL)),
        "g_post2": gain(ks[19], (DEPTH, D_MODEL)),
        "w_gu": nrm(ks[20], (DEPTH, D_MODEL, 2 * D_FF), D_MODEL ** -0.5),
        "w_down": nrm(ks[21], (DEPTH, D_FF, D_MODEL), D_FF ** -0.5),
    }


def reference(x_prompt, x_sample, c_prompt, c_sample, state_hgrn, cache_k, cache_v, page_table, hg_lb,
              w_ada, b_ada, g_pre1, g_post1, w_in, g_onorm, w_proj_a, w_proj_b, w_out, g_pre2, g_post2,
              w_gu, w_down):
    bp = x_prompt.shape[0]
    bs = x_sample.shape[0]
    n_pages = page_table.shape[1]
    lb_all = jnp.cumsum(jax.nn.softmax(hg_lb.astype(jnp.float32), axis=0), axis=0)
    xp, xs = x_prompt, x_sample
    sp_l, ss_l, kp_l, vp_l, ks_l, vs_l = [], [], [], [], [], []
    for l in range(DEPTH):
        w = (w_ada[l], b_ada[l], g_pre1[l], g_post1[l], w_in[l], g_onorm[l], w_proj_a[l], w_proj_b[l],
             w_out[l], g_pre2[l], g_post2[l], w_gu[l], w_down[l])
        s0 = jnp.zeros((bp, HG_HEADS, HG_DK, HG_DV), jnp.float32)
        empty = jnp.zeros((bp, 0, AT_HEADS, AT_DH), xp.dtype)
        xp, sp, kp, vp = hybrid_layer(xp, c_prompt, 0, s0, empty, empty, lb_all[l], *w)
        k_past = cache_k[l][page_table].reshape(bs, n_pages * PAGE_SIZE, AT_HEADS, AT_DH)
        v_past = cache_v[l][page_table].reshape(bs, n_pages * PAGE_SIZE, AT_HEADS, AT_DH)
        xs, ss, ksn, vsn = hybrid_layer(xs, c_sample, PAST_LEN, state_hgrn[l], k_past, v_past, lb_all[l], *w)
        sp_l.append(sp.astype(state_hgrn.dtype))
        ss_l.append(ss.astype(state_hgrn.dtype))
        kp_l.append(kp.astype(cache_k.dtype))
        vp_l.append(vp.astype(cache_v.dtype))
        ks_l.append(ksn.astype(cache_k.dtype))
        vs_l.append(vsn.astype(cache_v.dtype))
    return (xp, xs, jnp.stack(sp_l), jnp.stack(ss_l), jnp.stack(kp_l), jnp.stack(vp_l), jnp.stack(ks_l), jnp.stack(vs_l))
```

```python
import functools
import math

import jax
import jax.numpy as jnp
from jax import lax
from jax.experimental import pallas as pl
from jax.experimental.pallas import tpu as pltpu

F32 = jnp.float32
BF16 = jnp.bfloat16

HG_HEADS = 4
HG_DK = 128
HG_DV = 128
HG_CHUNK = 64
AT_HEADS = 8
AT_DH = 64
MOBA_BLOCK = 256
MOBA_TOPK = 3
ROPE_THETA = 10000.0
EPS = 1e-6
NEG = -1e30

V7X_VMEM_LIMIT_BYTES = 52 * 1024 * 1024
LANES = 128
HEADS_PER_VREG = LANES // AT_DH


def _cparams(sem):
    return pltpu.CompilerParams(dimension_semantics=sem, vmem_limit_bytes=V7X_VMEM_LIMIT_BYTES)


def _dot(a, b):
    return jnp.dot(a.astype(BF16), b.astype(BF16), preferred_element_type=F32)


def _dot_nt(a, b):
    return lax.dot_general(a.astype(BF16), b.astype(BF16), (((1,), (1,)), ((), ())), preferred_element_type=F32)


def _dot_tn(a, b):
    return lax.dot_general(a.astype(BF16), b.astype(BF16), (((0,), (0,)), ((), ())), preferred_element_type=F32)


def _split2(a):
    hi = a.astype(BF16)
    lo = (a - hi.astype(F32)).astype(BF16)
    return hi, lo


def _split3(a):
    p1 = a.astype(BF16)
    r1 = a - p1.astype(F32)
    p2 = r1.astype(BF16)
    p3 = (r1 - p2.astype(F32)).astype(BF16)
    return p1, p2, p3


def _dot3(a, b):
    ah, al = _split2(a)
    bh, bl = _split2(b)
    return _dot(ah, bh) + _dot(ah, bl) + _dot(al, bh)


def _dot3_nt(a, b):
    ah, al = _split2(a)
    bh, bl = _split2(b)
    return _dot_nt(ah, bh) + _dot_nt(ah, bl) + _dot_nt(al, bh)


def _dot_exact_rhs01(a, sel, pieces):
    parts = _split3(a)[:pieces] if pieces == 3 else _split2(a)
    out = _dot(parts[0], sel)
    for p in parts[1:]:
        out = out + _dot(p, sel)
    return out


def _sigmoid(x):
    return 1.0 / (1.0 + jnp.exp(-x))


def _rms(x, w):
    return x * lax.rsqrt(jnp.mean(x * x, axis=-1, keepdims=True) + EPS) * w


def _ada_kernel(c_ref, w_ref, b_ref, o_ref):
    c = c_ref[...]
    o_ref[...] = _dot3(c * _sigmoid(c), w_ref[...]) + b_ref[...]


def _ada_mod(c_all, w_ada, b_ada):
    rows, d = c_all.shape
    n = w_ada.shape[1]
    tn = 1024
    return pl.pallas_call(
        _ada_kernel,
        grid=(n // tn,),
        in_specs=[
            pl.BlockSpec((rows, d), lambda j: (0, 0)),
            pl.BlockSpec((d, tn), lambda j: (0, j)),
            pl.BlockSpec((1, tn), lambda j: (0, j)),
        ],
        out_specs=pl.BlockSpec((rows, tn), lambda j: (0, j)),
        out_shape=jax.ShapeDtypeStruct((rows, n), F32),
        compiler_params=_cparams(("arbitrary",)),
        name="ada_mod",
    )(c_all, w_ada, b_ada.reshape(1, n))


def _split_kernel(w_ref, hi_ref, lo_ref):
    hi, lo = _split2(w_ref[...])
    hi_ref[...] = hi
    lo_ref[...] = lo


def _split_weights(w):
    r, c = w.shape
    tr = 256
    return pl.pallas_call(
        _split_kernel,
        grid=(r // tr,),
        in_specs=[pl.BlockSpec((tr, c), lambda i: (i, 0))],
        out_specs=[pl.BlockSpec((tr, c), lambda i: (i, 0))] * 2,
        out_shape=[jax.ShapeDtypeStruct((r, c), BF16)] * 2,
        compiler_params=_cparams(("arbitrary",)),
        name="split_weights",
    )(w)


def _rope_group(y, cos, sin_signed):
    lane = lax.broadcasted_iota(jnp.int32, (1, LANES), 1)
    first_half = (lane % AT_DH) < (AT_DH // 2)
    rot = jnp.where(first_half, pltpu.roll(y, LANES - AT_DH // 2, 1), pltpu.roll(y, AT_DH // 2, 1))
    return y * cos + rot * sin_signed


def _in_proj_kernel(x_ref, sc_ref, sh_ref, g_ref, cos_ref, sin_ref, w_ref, wlo_ref,
                    hg_ref, q_ref, k_ref, v_ref, gate_ref, *, hgw, atw):
    x = x_ref[...]
    h = _rms(x, g_ref[...]) * (1.0 + sc_ref[0]) + sh_ref[0]
    hh, hl = _split2(h)
    seg = 512
    for j in range(hgw // seg):
        hg_ref[:, j * seg:(j + 1) * seg] = _dot(hh, w_ref[:, j * seg:(j + 1) * seg])
    for idx, out in enumerate((q_ref, k_ref)):
        c0 = hgw + idx * atw
        for g in range(atw // LANES):
            ws = w_ref[:, c0 + g * LANES:c0 + (g + 1) * LANES]
            wl = wlo_ref[:, idx * atw + g * LANES:idx * atw + (g + 1) * LANES]
            y = _dot(hh, ws) + _dot(hl, ws) + _dot(hh, wl)
            cs = cos_ref[:, g * LANES:(g + 1) * LANES]
            sn = sin_ref[:, g * LANES:(g + 1) * LANES]
            out[:, g * LANES:(g + 1) * LANES] = _rope_group(y, cs, sn)
    c0 = hgw + 2 * atw
    v_ref[...] = _dot(hh, w_ref[:, c0:c0 + atw])
    c0 = hgw + 3 * atw
    gw = gate_ref.shape[1]
    for j in range(gw // seg):
        gate_ref[:, j * seg:(j + 1) * seg] = _sigmoid(_dot(hh, w_ref[:, c0 + j * seg:c0 + (j + 1) * seg]))


def _in_proj(x, sc, sh, g_pre, cos_t, sin_t, w_hi, wqk_lo, tm):
    n, d = x.shape
    hgw = 2 * HG_HEADS * HG_DK + 2 * HG_HEADS * HG_DV
    atw = AT_HEADS * AT_DH
    gw = 2 * d
    ncols = w_hi.shape[1]
    assert ncols == hgw + 3 * atw + gw
    nt = n // tm
    nbm, r, _ = sc.shape
    tps = nt // nbm
    ctiles = cos_t.shape[0] // tm
    resident = dict(pipeline_mode=pl.Buffered(1))
    return pl.pallas_call(
        functools.partial(_in_proj_kernel, hgw=hgw, atw=atw),
        grid=(nt,),
        in_specs=[
            pl.BlockSpec((tm, d), lambda i: (i, 0)),
            pl.BlockSpec((1, r, d), lambda i: (i // tps, 0, 0)),
            pl.BlockSpec((1, r, d), lambda i: (i // tps, 0, 0)),
            pl.BlockSpec((1, d), lambda i: (0, 0)),
            pl.BlockSpec((tm, atw), lambda i: (i % ctiles, 0)),
            pl.BlockSpec((tm, atw), lambda i: (i % ctiles, 0)),
            pl.BlockSpec((d, ncols), lambda i: (0, 0), **resident),
            pl.BlockSpec((d, 2 * atw), lambda i: (0, 0), **resident),
        ],
        out_specs=[
            pl.BlockSpec((tm, hgw), lambda i: (i, 0)),
            pl.BlockSpec((tm, atw), lambda i: (i, 0)),
            pl.BlockSpec((tm, atw), lambda i: (i, 0)),
            pl.BlockSpec((tm, atw), lambda i: (i, 0)),
            pl.BlockSpec((tm, gw), lambda i: (i, 0)),
        ],
        out_shape=[
            jax.ShapeDtypeStruct((n, hgw), F32),
            jax.ShapeDtypeStruct((n, atw), F32),
            jax.ShapeDtypeStruct((n, atw), F32),
            jax.ShapeDtypeStruct((n, atw), F32),
            jax.ShapeDtypeStruct((n, gw), F32),
        ],
        compiler_params=_cparams(("arbitrary",)),
        name="in_proj",
    )(x, sc, sh, g_pre.reshape(1, d), cos_t, sin_t, w_hi, wqk_lo)


def _cumsum_rows(g):
    c = g.shape[0]
    row = lax.broadcasted_iota(jnp.int32, (c, c), 0)
    col = lax.broadcasted_iota(jnp.int32, (c, c), 1)
    tri = jnp.where(row >= col, 1.0, 0.0).astype(BF16)
    p1, p2, p3 = _split3(g)
    return (jnp.dot(tri, p1, preferred_element_type=F32) + jnp.dot(tri, p2, preferred_element_type=F32)
            + jnp.dot(tri, p3, preferred_element_type=F32))


def _hgrn_chunk(q, hf, v, hg, lb, gon, st, *, c, bd, valid):
    f = lb + (1.0 - lb) * _sigmoid(hf)
    gl = jnp.log(f)
    kin = 1.0 - f
    row = lax.broadcasted_iota(jnp.int32, (c, 1), 0)
    if valid < c:
        live = row < valid
        gl = jnp.where(live, gl, 0.0)
        kin = jnp.where(live, kin, 0.0)
    b = _cumsum_rows(gl)
    o = _dot_nt(q * jnp.exp(b), st)
    a = None
    m = c // 2
    while m >= bd:
        span = 2 * m
        right = (row % span) >= m
        npar = c // span
        ref = b[m - 1:m]
        for p in range(1, npar):
            ref = jnp.where(row // span == p, b[p * span + m - 1:p * span + m], ref)
        eq = jnp.exp(jnp.where(right, b - ref, NEG))
        ek = jnp.exp(jnp.where(right, NEG, ref - b))
        al = _dot_nt(q * eq, kin * ek)
        if npar > 1:
            rp = lax.broadcasted_iota(jnp.int32, (c, c), 0) // span
            cp = lax.broadcasted_iota(jnp.int32, (c, c), 1) // span
            al = jnp.where(rp == cp, al, 0.0)
        a = al if a is None else a + al
        m //= 2
    if a is not None:
        o = o + _dot(a, v)
    trow = lax.broadcasted_iota(jnp.int32, (bd, 1), 0)
    blocks = []
    for i in range(c // bd):
        sl = slice(i * bd, (i + 1) * bd)
        bi, qi, ki, vi = b[sl], q[sl], kin[sl], v[sl]
        od = jnp.zeros((bd, v.shape[1]), F32)
        for s in range(bd):
            e = jnp.exp(jnp.where(trow >= s, bi - bi[s:s + 1], NEG))
            ac = jnp.sum(qi * e * ki[s:s + 1], axis=-1, keepdims=True)
            od = od + ac * vi[s:s + 1]
        blocks.append(od)
    o = o + (blocks[0] if len(blocks) == 1 else jnp.concatenate(blocks, axis=0))
    bl = b[c - 1:c]
    st_new = st * jnp.exp(bl) + _dot_tn(v, kin * jnp.exp(bl - b))
    y = _rms(o, gon) * (hg * _sigmoid(hg))
    return y, st_new


def _hgrn_kernel(*refs, layer, c, bd, valid, n_inner, use_s0):
    if use_s0:
        lb_ref, gon_ref, q_ref, f_ref, i_ref, g_ref, s0_ref, o_ref, sout_ref, st_scr = refs
    else:
        lb_ref, gon_ref, q_ref, f_ref, i_ref, g_ref, o_ref, sout_ref, st_scr = refs
    cb = pl.program_id(2)

    @pl.when(cb == 0)
    def _init():
        if use_s0:
            st_scr[...] = s0_ref[0, 0].T
        else:
            st_scr[...] = jnp.zeros_like(st_scr)

    rows = [lb_ref[j:j + 1, :] for j in range(lb_ref.shape[0])]
    mx = rows[0]
    for r in rows[1:]:
        mx = jnp.maximum(mx, r)
    es = [jnp.exp(r - mx) for r in rows]
    tot = es[0]
    for e in es[1:]:
        tot = tot + e
    part = es[0]
    for e in es[1:layer + 1]:
        part = part + e
    lb = part / tot
    gon = gon_ref[...]

    def body(j, carry):
        r0 = pl.multiple_of(j * c, c)
        sl = pl.ds(r0, c)
        y, st_new = _hgrn_chunk(q_ref[sl, :], f_ref[sl, :], i_ref[sl, :], g_ref[sl, :], lb, gon, st_scr[...],
                                c=c, bd=bd, valid=valid)
        st_scr[...] = st_new
        o_ref[sl, :] = y
        return carry

    lax.fori_loop(0, n_inner, body, 0)

    @pl.when(cb == pl.num_programs(2) - 1)
    def _fin():
        sout_ref[0, 0] = st_scr[...].T


def _hgrn(hg4, hg_lb, g_onorm, s0, *, layer, bsz, t, c, bd, valid, rows_per_step):
    n = hg4.shape[0]
    h = HG_HEADS
    n_inner = rows_per_step // c
    steps = t // rows_per_step
    use_s0 = s0 is not None

    def col(k):
        return pl.BlockSpec((rows_per_step, HG_DK), lambda b, hh, cb, k=k: (b * steps + cb, k * h + hh))

    in_specs = [
        pl.BlockSpec((hg_lb.shape[0], HG_DK), lambda b, hh, cb: (0, hh)),
        pl.BlockSpec((1, HG_DV), lambda b, hh, cb: (0, 0)),
        col(0), col(1), col(2), col(3),
    ]
    args = [hg_lb, g_onorm.reshape(1, HG_DV), hg4, hg4, hg4, hg4]
    if use_s0:
        in_specs.append(pl.BlockSpec((1, 1, HG_DK, HG_DV), lambda b, hh, cb: (b, hh, 0, 0)))
        args.append(s0)
    return pl.pallas_call(
        functools.partial(_hgrn_kernel, layer=layer, c=c, bd=bd, valid=valid, n_inner=n_inner, use_s0=use_s0),
        grid=(bsz, h, steps),
        in_specs=in_specs,
        out_specs=[
            pl.BlockSpec((rows_per_step, HG_DV), lambda b, hh, cb: (b * steps + cb, hh)),
            pl.BlockSpec((1, 1, HG_DK, HG_DV), lambda b, hh, cb: (b, hh, 0, 0)),
        ],
        out_shape=[
            jax.ShapeDtypeStruct((n, h * HG_DV), F32),
            jax.ShapeDtypeStruct((bsz, h, HG_DK, HG_DV), F32),
        ],
        scratch_shapes=[pltpu.VMEM((HG_DV, HG_DK), F32)],
        compiler_params=_cparams(("arbitrary", "arbitrary", "arbitrary")),
        name="hgrn2",
    )(*args)


def _topk_select(gate, own, nblk):
    blk = lax.broadcasted_iota(jnp.int32, (1, LANES), 1)
    past = blk < own
    gm = jnp.where(past, gate, NEG)
    rank = jnp.zeros(gate.shape, F32)
    for m in range(nblk - 1):
        gcol = gm[:, m:m + 1]
        tie = jnp.where(blk > m, 1.0, 0.0)
        rank = rank + jnp.where(gcol > gm, 1.0, 0.0) + jnp.where(gcol == gm, tie, 0.0)
    return jnp.where(past, jnp.where(rank < MOBA_TOPK, 1.0, 0.0), 0.0)


def _attn_kernel(q_ref, k_ref, v_ref, o_ref, km_scr, *, tq, nblk):
    qi = pl.program_id(1)
    own = (qi * tq) // MOBA_BLOCK
    scale = AT_DH ** -0.5

    @pl.when(qi == 0)
    def _means():
        km_scr[...] = jnp.zeros_like(km_scr)
        for n in range(nblk):
            blk = k_ref[n * MOBA_BLOCK:(n + 1) * MOBA_BLOCK, :]
            km_scr[n:n + 1, :] = jnp.sum(blk, axis=0, keepdims=True) * (1.0 / MOBA_BLOCK)

    lane = lax.broadcasted_iota(jnp.int32, (1, LANES), 1)
    blkid = lane
    r0 = pl.multiple_of(own * MOBA_BLOCK, MOBA_BLOCK)
    qpos = qi * tq + lax.broadcasted_iota(jnp.int32, (tq, 1), 0)
    kpos_own = r0 + lax.broadcasted_iota(jnp.int32, (1, MOBA_BLOCK), 1)
    causal = kpos_own <= qpos

    for grp in range(AT_HEADS // HEADS_PER_VREG):
        lo = grp * LANES
        qp = q_ref[:, lo:lo + LANES]
        outs = []
        for sub in range(HEADS_PER_VREG):
            hm = (lane // AT_DH) == sub
            qh = jnp.where(hm, qp, 0.0) * scale
            gate = _dot3_nt(qh, km_scr[:, lo:lo + LANES])
            sel = _topk_select(gate, own, nblk)

            kb = k_ref[pl.ds(r0, MOBA_BLOCK), lo:lo + LANES]
            vb = v_ref[pl.ds(r0, MOBA_BLOCK), lo:lo + LANES]
            s = jnp.where(causal, _dot_nt(qh, kb), NEG)
            m0 = jnp.max(s, axis=-1, keepdims=True)
            p = jnp.exp(s - m0)
            l0 = jnp.sum(p, axis=-1, keepdims=True)
            acc0 = _dot(p, vb)

            def body(n, carry, qh=qh, sel=sel, lo=lo):
                m, l, acc = carry
                selcol = jnp.max(jnp.where(blkid == n, sel, 0.0), axis=-1, keepdims=True)
                rn = pl.multiple_of(n * MOBA_BLOCK, MOBA_BLOCK)
                kb = k_ref[pl.ds(rn, MOBA_BLOCK), lo:lo + LANES]
                vb = v_ref[pl.ds(rn, MOBA_BLOCK), lo:lo + LANES]
                s = jnp.where(selcol > 0.0, _dot_nt(qh, kb), NEG)
                mn = jnp.maximum(m, jnp.max(s, axis=-1, keepdims=True))
                al = jnp.exp(m - mn)
                p = jnp.exp(s - mn)
                return mn, al * l + jnp.sum(p, axis=-1, keepdims=True), al * acc + _dot(p, vb)

            m, l, acc = lax.fori_loop(0, own, body, (m0, l0, acc0))
            outs.append((hm, acc / l))
        o = outs[-1][1]
        for hm, oh in reversed(outs[:-1]):
            o = jnp.where(hm, oh, o)
        o_ref[:, lo:lo + LANES] = o


def _attn_prompt(q, k, v, *, bsz, t, tq):
    n, w = q.shape
    nblk = t // MOBA_BLOCK
    assert t % MOBA_BLOCK == 0 and MOBA_BLOCK % tq == 0 and nblk <= LANES
    nq = t // tq
    return pl.pallas_call(
        functools.partial(_attn_kernel, tq=tq, nblk=nblk),
        grid=(bsz, nq),
        in_specs=[
            pl.BlockSpec((tq, w), lambda b, i: (b * nq + i, 0)),
            pl.BlockSpec((t, w), lambda b, i: (b, 0)),
            pl.BlockSpec((t, w), lambda b, i: (b, 0)),
        ],
        out_specs=pl.BlockSpec((tq, w), lambda b, i: (b * nq + i, 0)),
        out_shape=jax.ShapeDtypeStruct((n, w), F32),
        scratch_shapes=[pltpu.VMEM((LANES, w), F32)],
        compiler_params=_cparams(("arbitrary", "arbitrary")),
        name="moba_prompt",
    )(q, k, v)


def _samp_scores_kernel(pt_ref, qm_ref, kn_ref, vn_ref, sel_ref, sel2_ref, kpage_ref, p_ref, oown_ref,
                        kacc_scr, s_scr, *, pages_per_blk, lchunk):
    del pt_ref
    pg = pl.program_id(1)
    npages = pl.num_programs(1)
    ps, nh, dh = kpage_ref.shape[2:]
    rows = qm_ref.shape[1]
    scale = dh ** -0.5

    @pl.when(pg == 0)
    def _zero():
        kacc_scr[...] = jnp.zeros_like(kacc_scr)

    kp3 = kpage_ref[0, 0]
    blk = pg // pages_per_blk
    kacc_scr[pl.ds(blk, 1)] = kacc_scr[pl.ds(blk, 1)] + jnp.sum(kp3, axis=0)[None]
    kp = kp3.reshape(ps * nh, dh)
    qm = qm_ref[0] * scale
    r8 = lax.broadcasted_iota(jnp.int32, (rows, 1), 0) % nh
    s_all = _dot_nt(qm, kp)
    c8 = lax.broadcasted_iota(jnp.int32, (1, ps * nh), 1) % nh
    s_m = jnp.where(r8 == c8, s_all, 0.0)
    s_scr[:, pl.ds(pl.multiple_of(pg * ps, ps), ps)] = _dot_exact_rhs01(s_m, sel_ref[...], 2)

    @pl.when(pg == npages - 1)
    def _finish():
        nblk = kacc_scr.shape[0]
        km = kacc_scr[...].reshape(nblk * nh, dh) * (1.0 / (pages_per_blk * ps))
        g_all = _dot3_nt(qm, km)
        cg = lax.broadcasted_iota(jnp.int32, (1, nblk * nh), 1) % nh
        gate = _dot_exact_rhs01(jnp.where(r8 == cg, g_all, 0.0), sel2_ref[...], 3)
        lane = lax.broadcasted_iota(jnp.int32, (1, nblk), 1).astype(F32)
        cur = gate
        picks = []
        for _ in range(min(MOBA_TOPK, nblk)):
            mx = jnp.max(cur, axis=-1, keepdims=True)
            idx = jnp.min(jnp.where(cur == mx, lane, float(nblk)), axis=-1, keepdims=True)
            picks.append(idx)
            cur = jnp.where(lane == idx, NEG, cur)
        so = _dot_nt(qm, kn_ref[0])
        ro = lax.broadcasted_iota(jnp.int32, (rows, 1), 0)
        co = lax.broadcasted_iota(jnp.int32, (1, rows), 1)
        own_ok = jnp.logical_and(ro % nh == co % nh, co // nh <= ro // nh)
        so = jnp.where(own_ok, so, NEG)
        m = jnp.max(so, axis=-1, keepdims=True)
        total = s_scr.shape[1]
        nch = total // lchunk
        bpc = lchunk // (pages_per_blk * ps)
        for cidx in range(nch):
            cs = slice(cidx * lchunk, (cidx + 1) * lchunk)
            bid = (cidx * bpc + lax.broadcasted_iota(jnp.int32, (1, lchunk), 1) // (pages_per_blk * ps)).astype(F32)
            hit = jnp.zeros((rows, lchunk), F32)
            for idx in picks:
                hit = jnp.where(bid == idx, 1.0, hit)
            sc = jnp.where(hit > 0.0, s_scr[:, cs], NEG)
            s_scr[:, cs] = sc
            m = jnp.maximum(m, jnp.max(sc, axis=-1, keepdims=True))
        po = jnp.exp(so - m)
        l = jnp.sum(po, axis=-1, keepdims=True)
        for cidx in range(nch):
            cs = slice(cidx * lchunk, (cidx + 1) * lchunk)
            pc = jnp.exp(s_scr[:, cs] - m)
            s_scr[:, cs] = pc
            l = l + jnp.sum(pc, axis=-1, keepdims=True)
        inv = 1.0 / l
        for cidx in range(nch):
            cs = slice(cidx * lchunk, (cidx + 1) * lchunk)
            p_ref[0, :, cs] = s_scr[:, cs] * inv
        oown_ref[0] = _dot(po * inv, vn_ref[0])


def _samp_pv_kernel(pt_ref, p_ref, oown_ref, selt_ref, vpage_ref, o_ref):
    del pt_ref
    pg = pl.program_id(1)
    ps, nh, dh = vpage_ref.shape[2:]
    rows = p_ref.shape[1]

    @pl.when(pg == 0)
    def _init():
        o_ref[0] = oown_ref[0]

    pw = p_ref[0]
    pe = _dot_exact_rhs01(pw, selt_ref[...], 2)
    r8 = lax.broadcasted_iota(jnp.int32, (rows, 1), 0) % nh
    c8 = lax.broadcasted_iota(jnp.int32, (1, ps * nh), 1) % nh
    pe = jnp.where(r8 == c8, pe, 0.0)
    o_ref[0] = o_ref[0] + _dot(pe, vpage_ref[0, 0].reshape(ps * nh, dh))


def _attn_sample(qm, kn, vn, cache_k, cache_v, page_table, *, layer):
    bs, rows, dh = qm.shape
    _, _, ps, nh, _ = cache_k.shape
    npages = page_table.shape[1]
    pages_per_blk = MOBA_BLOCK // ps
    assert MOBA_BLOCK % ps == 0 and npages % pages_per_blk == 0
    nblk = npages // pages_per_blk
    total = npages * ps
    tok_of_row = jnp.arange(ps * nh, dtype=jnp.int32) // nh
    sel = (tok_of_row[:, None] == jnp.arange(ps, dtype=jnp.int32)[None, :]).astype(BF16)
    blk_of_row = jnp.arange(nblk * nh, dtype=jnp.int32) // nh
    sel2 = (blk_of_row[:, None] == jnp.arange(nblk, dtype=jnp.int32)[None, :]).astype(BF16)
    page_spec = pl.BlockSpec((1, 1, ps, nh, dh), lambda b, p, pt: (layer, pt[b, p], 0, 0, 0))
    row_spec = pl.BlockSpec((1, rows, dh), lambda b, p, pt: (b, 0, 0))
    lchunk = 2048 if total % 2048 == 0 else total
    probs, o_own = pl.pallas_call(
        functools.partial(_samp_scores_kernel, pages_per_blk=pages_per_blk, lchunk=lchunk),
        grid_spec=pltpu.PrefetchScalarGridSpec(
            num_scalar_prefetch=1,
            grid=(bs, npages),
            in_specs=[
                row_spec, row_spec, row_spec,
                pl.BlockSpec(sel.shape, lambda b, p, pt: (0, 0)),
                pl.BlockSpec(sel2.shape, lambda b, p, pt: (0, 0)),
                page_spec,
            ],
            out_specs=[
                pl.BlockSpec((1, rows, total), lambda b, p, pt: (b, 0, 0)),
                row_spec,
            ],
            scratch_shapes=[pltpu.VMEM((nblk, nh, dh), F32), pltpu.VMEM((rows, total), F32)],
        ),
        out_shape=[jax.ShapeDtypeStruct((bs, rows, total), F32), jax.ShapeDtypeStruct((bs, rows, dh), F32)],
        compiler_params=_cparams(("arbitrary", "arbitrary")),
        name="moba_sample_scores",
    )(page_table, qm, kn, vn, sel, sel2, cache_k)
    return pl.pallas_call(
        _samp_pv_kernel,
        grid_spec=pltpu.PrefetchScalarGridSpec(
            num_scalar_prefetch=1,
            grid=(bs, npages),
            in_specs=[
                pl.BlockSpec((1, rows, ps), lambda b, p, pt: (b, 0, p)),
                row_spec,
                pl.BlockSpec((ps, ps * nh), lambda b, p, pt: (0, 0)),
                page_spec,
            ],
            out_specs=row_spec,
        ),
        out_shape=jax.ShapeDtypeStruct((bs, rows, dh), F32),
        compiler_params=_cparams(("arbitrary", "arbitrary")),
        name="moba_sample_pv",
    )(page_table, probs, o_own, sel.T, cache_v)


def _mix_kernel(x_ref, oa_ref, ob_ref, gate_ref, gt_ref, gpost_ref, wa_ref, wb_ref, wo_ref, out_ref):
    d = x_ref.shape[1]
    merged = gate_ref[:, :d] * _dot(oa_ref[...], wa_ref[...]) + gate_ref[:, d:] * _dot(ob_ref[...], wb_ref[...])
    y = _dot(merged, wo_ref[...])
    out_ref[...] = x_ref[...] + gt_ref[0] * _rms(y, gpost_ref[...])


def _mix(x, oa, ob, gates, gt, g_post, wa, wb, wo, tm):
    n, d = x.shape
    nt = n // tm
    nbm, r, _ = gt.shape
    tps = nt // nbm
    resident = dict(pipeline_mode=pl.Buffered(1))

    def rowspec(w):
        return pl.BlockSpec((tm, w), lambda i: (i, 0))

    return pl.pallas_call(
        _mix_kernel,
        grid=(nt,),
        in_specs=[
            rowspec(d), rowspec(oa.shape[1]), rowspec(ob.shape[1]), rowspec(gates.shape[1]),
            pl.BlockSpec((1, r, d), lambda i: (i // tps, 0, 0)),
            pl.BlockSpec((1, d), lambda i: (0, 0)),
            pl.BlockSpec(wa.shape, lambda i: (0, 0), **resident),
            pl.BlockSpec(wb.shape, lambda i: (0, 0), **resident),
            pl.BlockSpec(wo.shape, lambda i: (0, 0), **resident),
        ],
        out_specs=rowspec(d),
        out_shape=jax.ShapeDtypeStruct((n, d), F32),
        compiler_params=_cparams(("arbitrary",)),
        name="merge_out_proj",
    )(x, oa, ob, gates, gt, g_post.reshape(1, d), wa, wb, wo)


def _ffn_kernel(x_ref, sc_ref, sh_ref, gt_ref, gpre_ref, gpost_ref, wgu_ref, wd_ref, out_ref, *, dff, fchunk):
    x = x_ref[...]
    hb = (_rms(x, gpre_ref[...]) * (1.0 + sc_ref[0]) + sh_ref[0]).astype(BF16)
    acc = jnp.zeros(x.shape, F32)
    for j in range(dff // fchunk):
        g = _dot(hb, wgu_ref[:, j * fchunk:(j + 1) * fchunk])
        u = _dot(hb, wgu_ref[:, dff + j * fchunk:dff + (j + 1) * fchunk])
        acc = acc + _dot(g * _sigmoid(g) * u, wd_ref[j * fchunk:(j + 1) * fchunk, :])
    out_ref[...] = x + gt_ref[0] * _rms(acc, gpost_ref[...])


def _ffn(x, sc, sh, gt, g_pre, g_post, wgu, wd, tm):
    n, d = x.shape
    dff = wd.shape[0]
    nt = n // tm
    nbm, r, _ = gt.shape
    tps = nt // nbm
    fchunk = dff // 2 if (dff // 2) % LANES == 0 else dff
    resident = dict(pipeline_mode=pl.Buffered(1))
    modspec = pl.BlockSpec((1, r, d), lambda i: (i // tps, 0, 0))
    vecspec = pl.BlockSpec((1, d), lambda i: (0, 0))
    return pl.pallas_call(
        functools.partial(_ffn_kernel, dff=dff, fchunk=fchunk),
        grid=(nt,),
        in_specs=[
            pl.BlockSpec((tm, d), lambda i: (i, 0)),
            modspec, modspec, modspec, vecspec, vecspec,
            pl.BlockSpec(wgu.shape, lambda i: (0, 0), **resident),
            pl.BlockSpec(wd.shape, lambda i: (0, 0), **resident),
        ],
        out_specs=pl.BlockSpec((tm, d), lambda i: (i, 0)),
        out_shape=jax.ShapeDtypeStruct((n, d), F32),
        compiler_params=_cparams(("arbitrary",)),
        name="swiglu_ffn",
    )(x, sc, sh, gt, g_pre.reshape(1, d), g_post.reshape(1, d), wgu, wd)


def _rope_tables(pos):
    half = AT_DH // 2
    inv = ROPE_THETA ** (-jnp.arange(half, dtype=F32) / half)
    ang = pos.astype(F32)[:, None] * inv[None, :]
    cos = jnp.cos(ang)
    sin = jnp.sin(ang)
    cos_h = jnp.concatenate([cos, cos], axis=-1)
    sin_h = jnp.concatenate([-sin, sin], axis=-1)
    return jnp.tile(cos_h, (1, AT_HEADS)), jnp.tile(sin_h, (1, AT_HEADS))


def _group_layer(x2d, mods, *, bsz, t, pos0, s0, paged, layer, weights, tm, hg_cfg):
    sh1, sc1, gt1, sh2, sc2, gt2 = mods
    (hg_lb, g_pre1, g_post1, w_in_hi, wqk_lo, g_onorm, wa, wb, wo, g_pre2, g_post2, wgu, wd) = weights
    n = x2d.shape[0]
    cos_t, sin_t = _rope_tables(pos0 + jnp.arange(t, dtype=jnp.int32))
    if tm > t:
        cos_t = jnp.tile(cos_t, (tm // t, 1))
        sin_t = jnp.tile(sin_t, (tm // t, 1))
    hg4, q, k, v, gates = _in_proj(x2d, sc1, sh1, g_pre1, cos_t, sin_t, w_in_hi, wqk_lo, tm)

    c, bd, rows_per_step = hg_cfg
    tp = -(-t // c) * c
    if tp != t:
        hg4p = jnp.pad(hg4.reshape(bsz, t, -1), ((0, 0), (0, tp - t), (0, 0))).reshape(bsz * tp, -1)
    else:
        hg4p = hg4
    o_a, s_new = _hgrn(hg4p, hg_lb, g_onorm, s0, layer=layer, bsz=bsz, t=tp, c=c, bd=bd,
                       valid=c - (tp - t), rows_per_step=rows_per_step)
    if tp != t:
        o_a = o_a.reshape(bsz, tp, -1)[:, :t].reshape(n, -1)

    if paged is None:
        o_b = _attn_prompt(q, k, v, bsz=bsz, t=t, tq=128)
    else:
        cache_k, cache_v, page_table = paged
        to_rows = lambda a: a.reshape(bsz, t * AT_HEADS, AT_DH)
        o_b = _attn_sample(to_rows(q), to_rows(k), to_rows(v), cache_k, cache_v, page_table, layer=layer)
        o_b = o_b.reshape(n, AT_HEADS * AT_DH)

    x1 = _mix(x2d, o_a, o_b, gates, gt1, g_post1, wa, wb, wo, tm)
    x2 = _ffn(x1, sc2, sh2, gt2, g_pre2, g_post2, wgu, wd, tm)
    return x2, s_new, k, v


def kernel(x_prompt, x_sample, c_prompt, c_sample, state_hgrn, cache_k, cache_v, page_table, hg_lb, w_ada, b_ada,
           g_pre1, g_post1, w_in, g_onorm, w_proj_a, w_proj_b, w_out, g_pre2, g_post2, w_gu, w_down):
    bp, t, d = x_prompt.shape
    bs, ts, _ = x_sample.shape
    depth = w_in.shape[0]
    past_len = page_table.shape[1] * cache_k.shape[2]
    assert past_len % MOBA_BLOCK == 0 and ts <= MOBA_BLOCK
    hgw = 2 * HG_HEADS * HG_DK + 2 * HG_HEADS * HG_DV
    atw = AT_HEADS * AT_DH

    xp = x_prompt.reshape(bp * t, d)
    xs = x_sample.reshape(bs * ts, d)
    c_all = jnp.concatenate([c_prompt, c_sample], axis=0)
    outs = [[] for _ in range(6)]
    for l in range(depth):
        mod = _ada_mod(c_all, w_ada[l], b_ada[l])
        mods_p = [m[:bp].reshape(bp, 1, d) for m in jnp.split(mod, 6, axis=-1)]
        mods_s = [jnp.repeat(m[bp:], ts, axis=0).reshape(1, bs * ts, d) for m in jnp.split(mod, 6, axis=-1)]
        _, wqk_lo = _split_weights(w_in[l][:, hgw:hgw + 2 * atw])
        weights = (hg_lb, g_pre1[l], g_post1[l], w_in[l].astype(BF16), wqk_lo, g_onorm[l],
                   w_proj_a[l].astype(BF16), w_proj_b[l].astype(BF16), w_out[l].astype(BF16),
                   g_pre2[l], g_post2[l], w_gu[l].astype(BF16), w_down[l].astype(BF16))
        chunk = HG_CHUNK if t % HG_CHUNK == 0 else t
        xp, sp, kp, vp = _group_layer(xp, mods_p, bsz=bp, t=t, pos0=0, s0=None, paged=None, layer=l,
                                      weights=weights, tm=256, hg_cfg=(chunk, 16, 256))
        cs = -(-ts // 8) * 8
        xs, ss, ksn, vsn = _group_layer(xs, mods_s, bsz=bs, t=ts, pos0=past_len, s0=state_hgrn[l],
                                        paged=(cache_k, cache_v, page_table), layer=l,
                                        weights=weights, tm=bs * ts, hg_cfg=(cs, cs, cs))
        for lst, val in zip(outs, (sp, ss,
                                   kp.reshape(bp, t, AT_HEADS, AT_DH), vp.reshape(bp, t, AT_HEADS, AT_DH),
                                   ksn.reshape(bs, ts, AT_HEADS, AT_DH), vsn.reshape(bs, ts, AT_HEADS, AT_DH))):
            lst.append(val)
    return (xp.reshape(bp, t, d), xs.reshape(bs, ts, d)) + tuple(jnp.stack(o) for o in outs)
```

```python
import functools
import math

import jax
import jax.numpy as jnp
from jax import lax
from jax.experimental import pallas as pl
from jax.experimental.pallas import tpu as pltpu

F32 = jnp.float32
BF16 = jnp.bfloat16

HG_HEADS = 4
HG_DK = 128
HG_DV = 128
HG_CHUNK = 64
AT_HEADS = 8
AT_DH = 64
MOBA_BLOCK = 256
MOBA_TOPK = 3
ROPE_THETA = 10000.0
EPS = 1e-6
NEG = -1e30

V7X_VMEM_LIMIT_BYTES = 52 * 1024 * 1024
LANES = 128
SUBLANES = 8
HEADS_PER_VREG = LANES // AT_DH
PAGES_PER_STEP = 16


def _cparams(sem):
    return pltpu.CompilerParams(dimension_semantics=sem, vmem_limit_bytes=V7X_VMEM_LIMIT_BYTES)


def _dot(a, b):
    return jnp.dot(a.astype(BF16), b.astype(BF16), preferred_element_type=F32)


def _dot_nt(a, b):
    return lax.dot_general(a.astype(BF16), b.astype(BF16), (((1,), (1,)), ((), ())), preferred_element_type=F32)


def _dot_tn(a, b):
    return lax.dot_general(a.astype(BF16), b.astype(BF16), (((0,), (0,)), ((), ())), preferred_element_type=F32)


def _split2(a):
    hi = a.astype(BF16)
    lo = (a - hi.astype(F32)).astype(BF16)
    return hi, lo


def _split3(a):
    p1 = a.astype(BF16)
    r1 = a - p1.astype(F32)
    p2 = r1.astype(BF16)
    p3 = (r1 - p2.astype(F32)).astype(BF16)
    return p1, p2, p3


def _dot3(a, b):
    ah, al = _split2(a)
    bh, bl = _split2(b)
    return _dot(ah, bh) + _dot(ah, bl) + _dot(al, bh)


def _dot3_nt(a, b):
    ah, al = _split2(a)
    bh, bl = _split2(b)
    return _dot_nt(ah, bh) + _dot_nt(ah, bl) + _dot_nt(al, bh)


def _sigmoid(x):
    return 1.0 / (1.0 + jnp.exp(-x))


def _rms(x, w):
    return x * lax.rsqrt(jnp.mean(x * x, axis=-1, keepdims=True) + EPS) * w


def _ada_kernel(c_ref, w_ref, b_ref, o_ref):
    c = c_ref[...]
    o_ref[...] = _dot3(c * _sigmoid(c), w_ref[...]) + b_ref[...]


def _ada_mod(c_all, w_ada, b_ada):
    rows, d = c_all.shape
    n = w_ada.shape[1]
    tn = 1024
    return pl.pallas_call(
        _ada_kernel,
        grid=(n // tn,),
        in_specs=[
            pl.BlockSpec((rows, d), lambda j: (0, 0)),
            pl.BlockSpec((d, tn), lambda j: (0, j)),
            pl.BlockSpec((1, tn), lambda j: (0, j)),
        ],
        out_specs=pl.BlockSpec((rows, tn), lambda j: (0, j)),
        out_shape=jax.ShapeDtypeStruct((rows, n), F32),
        compiler_params=_cparams(("arbitrary",)),
        name="ada_mod",
    )(c_all, w_ada, b_ada.reshape(1, n))


def _split_kernel(w_ref, hi_ref, lo_ref):
    hi, lo = _split2(w_ref[...])
    hi_ref[...] = hi
    lo_ref[...] = lo


def _split_weights(w):
    r, c = w.shape
    tr = 256
    return pl.pallas_call(
        _split_kernel,
        grid=(r // tr,),
        in_specs=[pl.BlockSpec((tr, c), lambda i: (i, 0))],
        out_specs=[pl.BlockSpec((tr, c), lambda i: (i, 0))] * 2,
        out_shape=[jax.ShapeDtypeStruct((r, c), BF16)] * 2,
        compiler_params=_cparams(("arbitrary",)),
        name="split_weights",
    )(w)


def _rope_group(y, cos, sin_signed):
    lane = lax.broadcasted_iota(jnp.int32, (1, LANES), 1)
    first_half = (lane % AT_DH) < (AT_DH // 2)
    rot = jnp.where(first_half, pltpu.roll(y, LANES - AT_DH // 2, 1), pltpu.roll(y, AT_DH // 2, 1))
    return y * cos + rot * sin_signed


def _in_proj_kernel(x_ref, sc_ref, sh_ref, g_ref, cos_ref, sin_ref, w_ref, wlo_ref, *outs, hgw, atw, kv_t):
    if kv_t:
        hg_ref, q_ref, k_ref, kt_ref, vt_ref, gate_ref = outs
    else:
        hg_ref, q_ref, k_ref, v_ref, gate_ref = outs
    x = x_ref[...]
    h = _rms(x, g_ref[...]) * (1.0 + sc_ref[0]) + sh_ref[0]
    hh, hl = _split2(h)
    seg = 512
    for j in range(hgw // seg):
        hg_ref[:, j * seg:(j + 1) * seg] = _dot(hh, w_ref[:, j * seg:(j + 1) * seg])
    for idx, out in enumerate((q_ref, k_ref)):
        c0 = hgw + idx * atw
        for g in range(atw // LANES):
            ws = w_ref[:, c0 + g * LANES:c0 + (g + 1) * LANES]
            wl = wlo_ref[:, idx * atw + g * LANES:idx * atw + (g + 1) * LANES]
            y = _dot(hh, ws) + _dot(hl, ws) + _dot(hh, wl)
            cs = cos_ref[:, g * LANES:(g + 1) * LANES]
            sn = sin_ref[:, g * LANES:(g + 1) * LANES]
            y = _rope_group(y, cs, sn)
            out[:, g * LANES:(g + 1) * LANES] = y
            if kv_t and idx == 1:
                kt_ref[0, g * LANES:(g + 1) * LANES, :] = y.T
    c0 = hgw + 2 * atw
    if kv_t:
        for g in range(atw // LANES):
            vt_ref[0, g * LANES:(g + 1) * LANES, :] = _dot(hh, w_ref[:, c0 + g * LANES:c0 + (g + 1) * LANES]).T
    else:
        v_ref[...] = _dot(hh, w_ref[:, c0:c0 + atw])
    c0 = hgw + 3 * atw
    gw = gate_ref.shape[1]
    for j in range(gw // seg):
        gate_ref[:, j * seg:(j + 1) * seg] = _sigmoid(_dot(hh, w_ref[:, c0 + j * seg:c0 + (j + 1) * seg]))


def _in_proj(x, sc, sh, g_pre, cos_t, sin_t, w_hi, wqk_lo, tm, seq_len=None):
    n, d = x.shape
    kv_t = seq_len is not None
    hgw = 2 * HG_HEADS * HG_DK + 2 * HG_HEADS * HG_DV
    atw = AT_HEADS * AT_DH
    gw = 2 * d
    ncols = w_hi.shape[1]
    assert ncols == hgw + 3 * atw + gw
    nt = n // tm
    nbm, r, _ = sc.shape
    tps = nt // nbm
    ctiles = cos_t.shape[0] // tm
    resident = dict(pipeline_mode=pl.Buffered(1))
    rowspec = lambda w: pl.BlockSpec((tm, w), lambda i: (i, 0))
    rowshape = lambda w: jax.ShapeDtypeStruct((n, w), F32)
    if kv_t:
        tiles_per_seq = seq_len // tm
        tspec = pl.BlockSpec((1, atw, tm), lambda i: (i // tiles_per_seq, 0, i % tiles_per_seq))
        tshape = jax.ShapeDtypeStruct((n // seq_len, atw, seq_len), F32)
        kv_specs, kv_shapes = [rowspec(atw), tspec, tspec], [rowshape(atw), tshape, tshape]
    else:
        kv_specs, kv_shapes = [rowspec(atw), rowspec(atw)], [rowshape(atw), rowshape(atw)]
    return pl.pallas_call(
        functools.partial(_in_proj_kernel, hgw=hgw, atw=atw, kv_t=kv_t),
        grid=(nt,),
        in_specs=[
            pl.BlockSpec((tm, d), lambda i: (i, 0)),
            pl.BlockSpec((1, r, d), lambda i: (i // tps, 0, 0)),
            pl.BlockSpec((1, r, d), lambda i: (i // tps, 0, 0)),
            pl.BlockSpec((1, d), lambda i: (0, 0)),
            pl.BlockSpec((tm, atw), lambda i: (i % ctiles, 0)),
            pl.BlockSpec((tm, atw), lambda i: (i % ctiles, 0)),
            pl.BlockSpec((d, ncols), lambda i: (0, 0), **resident),
            pl.BlockSpec((d, 2 * atw), lambda i: (0, 0), **resident),
        ],
        out_specs=[rowspec(hgw), rowspec(atw)] + kv_specs + [rowspec(gw)],
        out_shape=[rowshape(hgw), rowshape(atw)] + kv_shapes + [rowshape(gw)],
        compiler_params=_cparams(("arbitrary",)),
        name="in_proj",
    )(x, sc, sh, g_pre.reshape(1, d), cos_t, sin_t, w_hi, wqk_lo)


def _cumsum_rows(g):
    c = g.shape[0]
    row = lax.broadcasted_iota(jnp.int32, (c, c), 0)
    col = lax.broadcasted_iota(jnp.int32, (c, c), 1)
    tri = jnp.where(row >= col, 1.0, 0.0).astype(BF16)
    p1, p2, p3 = _split3(g)
    return (jnp.dot(tri, p1, preferred_element_type=F32) + jnp.dot(tri, p2, preferred_element_type=F32)
            + jnp.dot(tri, p3, preferred_element_type=F32))


def _hgrn_chunk(q, hf, v, hg, lb, gon, st, *, c, bd, valid):
    f = lb + (1.0 - lb) * _sigmoid(hf)
    gl = jnp.log(f)
    kin = 1.0 - f
    row = lax.broadcasted_iota(jnp.int32, (c, 1), 0)
    if valid < c:
        live = row < valid
        gl = jnp.where(live, gl, 0.0)
        kin = jnp.where(live, kin, 0.0)
    b = _cumsum_rows(gl)
    o = _dot_nt(q * jnp.exp(b), st)
    a = None
    m = c // 2
    while m >= bd:
        span = 2 * m
        right = (row % span) >= m
        npar = c // span
        ref = b[m - 1:m]
        for p in range(1, npar):
            ref = jnp.where(row // span == p, b[p * span + m - 1:p * span + m], ref)
        eq = jnp.exp(jnp.where(right, b - ref, NEG))
        ek = jnp.exp(jnp.where(right, NEG, ref - b))
        al = _dot_nt(q * eq, kin * ek)
        if npar > 1:
            rp = lax.broadcasted_iota(jnp.int32, (c, c), 0) // span
            cp = lax.broadcasted_iota(jnp.int32, (c, c), 1) // span
            al = jnp.where(rp == cp, al, 0.0)
        a = al if a is None else a + al
        m //= 2
    if a is not None:
        o = o + _dot(a, v)
    trow = lax.broadcasted_iota(jnp.int32, (bd, 1), 0)
    blocks = []
    for i in range(c // bd):
        sl = slice(i * bd, (i + 1) * bd)
        bi, qi, ki, vi = b[sl], q[sl], kin[sl], v[sl]
        od = jnp.zeros((bd, v.shape[1]), F32)
        for s in range(bd):
            e = jnp.exp(jnp.where(trow >= s, bi - bi[s:s + 1], NEG))
            ac = jnp.sum(qi * e * ki[s:s + 1], axis=-1, keepdims=True)
            od = od + ac * vi[s:s + 1]
        blocks.append(od)
    o = o + (blocks[0] if len(blocks) == 1 else jnp.concatenate(blocks, axis=0))
    bl = b[c - 1:c]
    st_new = st * jnp.exp(bl) + _dot_tn(v, kin * jnp.exp(bl - b))
    y = _rms(o, gon) * (hg * _sigmoid(hg))
    return y, st_new


def _hgrn_kernel(*refs, layer, c, bd, valid, n_inner, use_s0):
    if use_s0:
        lb_ref, gon_ref, q_ref, f_ref, i_ref, g_ref, s0_ref, o_ref, sout_ref, st_scr = refs
    else:
        lb_ref, gon_ref, q_ref, f_ref, i_ref, g_ref, o_ref, sout_ref, st_scr = refs
    cb = pl.program_id(2)

    @pl.when(cb == 0)
    def _init():
        if use_s0:
            st_scr[...] = s0_ref[0, 0].T
        else:
            st_scr[...] = jnp.zeros_like(st_scr)

    rows = [lb_ref[j:j + 1, :] for j in range(lb_ref.shape[0])]
    mx = rows[0]
    for r in rows[1:]:
        mx = jnp.maximum(mx, r)
    es = [jnp.exp(r - mx) for r in rows]
    tot = es[0]
    for e in es[1:]:
        tot = tot + e
    part = es[0]
    for e in es[1:layer + 1]:
        part = part + e
    lb = part / tot
    gon = gon_ref[...]

    st = st_scr[...]
    for j in range(n_inner):
        sl = slice(j * c, (j + 1) * c)
        y, st = _hgrn_chunk(q_ref[sl, :], f_ref[sl, :], i_ref[sl, :], g_ref[sl, :], lb, gon, st,
                            c=c, bd=bd, valid=valid)
        o_ref[sl, :] = y
    st_scr[...] = st

    @pl.when(cb == pl.num_programs(2) - 1)
    def _fin():
        sout_ref[0, 0] = st_scr[...].T


def _hgrn(hg4, hg_lb, g_onorm, s0, *, layer, bsz, t, c, bd, valid, rows_per_step):
    n = hg4.shape[0]
    h = HG_HEADS
    n_inner = rows_per_step // c
    steps = t // rows_per_step
    use_s0 = s0 is not None

    def col(k):
        return pl.BlockSpec((rows_per_step, HG_DK), lambda b, hh, cb, k=k: (b * steps + cb, k * h + hh))

    in_specs = [
        pl.BlockSpec((hg_lb.shape[0], HG_DK), lambda b, hh, cb: (0, hh)),
        pl.BlockSpec((1, HG_DV), lambda b, hh, cb: (0, 0)),
        col(0), col(1), col(2), col(3),
    ]
    args = [hg_lb, g_onorm.reshape(1, HG_DV), hg4, hg4, hg4, hg4]
    if use_s0:
        in_specs.append(pl.BlockSpec((1, 1, HG_DK, HG_DV), lambda b, hh, cb: (b, hh, 0, 0)))
        args.append(s0)
    return pl.pallas_call(
        functools.partial(_hgrn_kernel, layer=layer, c=c, bd=bd, valid=valid, n_inner=n_inner, use_s0=use_s0),
        grid=(bsz, h, steps),
        in_specs=in_specs,
        out_specs=[
            pl.BlockSpec((rows_per_step, HG_DV), lambda b, hh, cb: (b * steps + cb, hh)),
            pl.BlockSpec((1, 1, HG_DK, HG_DV), lambda b, hh, cb: (b, hh, 0, 0)),
        ],
        out_shape=[
            jax.ShapeDtypeStruct((n, h * HG_DV), F32),
            jax.ShapeDtypeStruct((bsz, h, HG_DK, HG_DV), F32),
        ],
        scratch_shapes=[pltpu.VMEM((HG_DV, HG_DK), F32)],
        compiler_params=_cparams(("arbitrary", "arbitrary", "arbitrary")),
        name="hgrn2",
    )(*args)


def _topk_select_t(g, own, nblk):
    rown = lax.broadcasted_iota(jnp.int32, (g.shape[0], 1), 0)
    past = rown < own
    gm = jnp.where(past, g, NEG)
    rank = jnp.zeros(g.shape, F32)
    for m in range(nblk - 1):
        grow = gm[m:m + 1, :]
        tie = jnp.where(rown > m, 1.0, 0.0)
        rank = rank + jnp.where(grow > gm, 1.0, 0.0) + jnp.where(grow == gm, tie, 0.0)
    return jnp.where(past, jnp.where(rank < MOBA_TOPK, 1.0, 0.0), 0.0)


def _attn_kernel(q_ref, k_ref, vt_ref, o_ref, km_scr, s_scr, *, nblk):
    tq = MOBA_BLOCK
    own = pl.program_id(1)
    scale = AT_DH ** -0.5

    @pl.when(own == 0)
    def _means():
        km_scr[...] = jnp.zeros_like(km_scr)
        for n in range(nblk):
            blk = k_ref[n * MOBA_BLOCK:(n + 1) * MOBA_BLOCK, :]
            km_scr[n:n + 1, :] = jnp.sum(blk, axis=0, keepdims=True) * (1.0 / MOBA_BLOCK)

    lane = lax.broadcasted_iota(jnp.int32, (1, LANES), 1)
    rown = lax.broadcasted_iota(jnp.int32, (km_scr.shape[0], 1), 0)
    causal = (lax.broadcasted_iota(jnp.int32, (MOBA_BLOCK, tq), 0)
              <= lax.broadcasted_iota(jnp.int32, (MOBA_BLOCK, tq), 1))
    r_own = pl.multiple_of(own * MOBA_BLOCK, MOBA_BLOCK)

    for grp in range(AT_HEADS // HEADS_PER_VREG):
        lo = grp * LANES
        qp = q_ref[:, lo:lo + LANES]
        kmp = km_scr[:, lo:lo + LANES]
        qhs, sels, ms = [], [], []
        kb = k_ref[pl.ds(r_own, MOBA_BLOCK), lo:lo + LANES]
        for sub in range(HEADS_PER_VREG):
            qh = (jnp.where((lane // AT_DH) == sub, qp, 0.0) * scale)
            sels.append(_topk_select_t(_dot3_nt(kmp, qh), own, nblk))
            qh = qh.astype(BF16)
            qhs.append(qh)
            st = jnp.where(causal, _dot_nt(kb, qh), NEG)
            s_scr[sub, own] = st
            ms.append(jnp.max(st, axis=0, keepdims=True))

        def pass_a(n, ms, qhs=qhs, sels=sels, lo=lo):
            kb = k_ref[pl.ds(pl.multiple_of(n * MOBA_BLOCK, MOBA_BLOCK), MOBA_BLOCK), lo:lo + LANES]
            out = []
            for sub in range(HEADS_PER_VREG):
                selrow = jnp.max(jnp.where(rown == n, sels[sub], 0.0), axis=0, keepdims=True)
                st = jnp.where(selrow > 0.0, _dot_nt(kb, qhs[sub]), NEG)
                s_scr[sub, n] = st
                out.append(jnp.maximum(ms[sub], jnp.max(st, axis=0, keepdims=True)))
            return tuple(out)

        ms = lax.fori_loop(0, own, pass_a, tuple(ms))

        def pass_b(n, carry, ms=ms, lo=lo):
            cols = pl.ds(pl.multiple_of(n * MOBA_BLOCK, MOBA_BLOCK), MOBA_BLOCK)
            out = []
            for sub in range(HEADS_PER_VREG):
                l, acc = carry[sub]
                p = jnp.exp(s_scr[sub, n] - ms[sub])
                vtb = vt_ref[0, lo + sub * AT_DH:lo + (sub + 1) * AT_DH, cols]
                out.append((l + jnp.sum(p, axis=0, keepdims=True), acc + _dot(vtb, p)))
            return tuple(out)

        init = tuple((jnp.zeros((1, tq), F32), jnp.zeros((AT_DH, tq), F32)) for _ in range(HEADS_PER_VREG))
        res = lax.fori_loop(0, own + 1, pass_b, init)
        ot = jnp.concatenate([acc / l for l, acc in res], axis=0)
        o_ref[:, lo:lo + LANES] = ot.T


def _attn_prompt(q, k, vt, *, bsz, t):
    n, w = q.shape
    nblk = t // MOBA_BLOCK
    assert t % MOBA_BLOCK == 0
    rpad = -(-nblk // SUBLANES) * SUBLANES
    return pl.pallas_call(
        functools.partial(_attn_kernel, nblk=nblk),
        grid=(bsz, nblk),
        in_specs=[
            pl.BlockSpec((MOBA_BLOCK, w), lambda b, i: (b * nblk + i, 0)),
            pl.BlockSpec((t, w), lambda b, i: (b, 0)),
            pl.BlockSpec((1, w, t), lambda b, i: (b, 0, 0)),
        ],
        out_specs=pl.BlockSpec((MOBA_BLOCK, w), lambda b, i: (b * nblk + i, 0)),
        out_shape=jax.ShapeDtypeStruct((n, w), F32),
        scratch_shapes=[pltpu.VMEM((rpad, w), F32),
                        pltpu.VMEM((HEADS_PER_VREG, nblk, MOBA_BLOCK, MOBA_BLOCK), F32)],
        compiler_params=_cparams(("arbitrary", "arbitrary")),
        name="moba_prompt",
    )(q, k, vt)


def _samp_scores_kernel(*refs, npg, pages_per_blk):
    pt_ref, qbd_ref, qm_ref, kn_ref, vn_ref = refs[:5]
    kpages = refs[5:5 + npg]
    p_ref, oown_ref, s_scr = refs[5 + npg:]
    del pt_ref
    g = pl.program_id(1)
    nh, dh, ps = kpages[0].shape[2:]
    rows = qm_ref.shape[1]
    scale = dh ** -0.5

    qh, ql = _split2(qbd_ref[0] * scale)
    qs = jnp.concatenate([qh, ql], axis=0)
    for j in range(npg):
        kh, kl = _split2(kpages[j][0, 0].reshape(nh * dh, ps))
        a = jnp.dot(qs, kh, preferred_element_type=F32)
        s = a[:rows] + a[rows:] + jnp.dot(qh, kl, preferred_element_type=F32)
        s_scr[:, pl.ds(pl.multiple_of((g * npg + j) * ps, ps), ps)] = s

    @pl.when(g == pl.num_programs(1) - 1)
    def _finish():
        total = s_scr.shape[1]
        bw = pages_per_blk * ps
        nblk = total // bw
        lane = lax.broadcasted_iota(jnp.int32, (1, LANES), 1)
        gate = jnp.full((rows, LANES), NEG, F32)
        for n in range(nblk):
            col = jnp.sum(s_scr[:, n * bw:(n + 1) * bw], axis=-1, keepdims=True) * (1.0 / bw)
            gate = jnp.where(lane == n, col, gate)
        lanef = lane.astype(F32)
        cur = gate
        picks = []
        for _ in range(min(MOBA_TOPK, nblk)):
            mx = jnp.max(cur, axis=-1, keepdims=True)
            idx = jnp.min(jnp.where(cur == mx, lanef, float(LANES)), axis=-1, keepdims=True)
            picks.append(idx)
            cur = jnp.where(lanef == idx, NEG, cur)
        qm = qm_ref[0] * scale
        so = _dot_nt(qm, kn_ref[0])
        ro = lax.broadcasted_iota(jnp.int32, (rows, 1), 0)
        co = lax.broadcasted_iota(jnp.int32, (1, rows), 1)
        own_ok = jnp.logical_and(ro % nh == co % nh, co // nh <= ro // nh)
        so = jnp.where(own_ok, so, NEG)
        m = jnp.max(so, axis=-1, keepdims=True)
        lchunk = 8 * bw if total % (8 * bw) == 0 else bw
        nch = total // lchunk
        for cidx in range(nch):
            cs = slice(cidx * lchunk, (cidx + 1) * lchunk)
            bid = (cidx * (lchunk // bw) + lax.broadcasted_iota(jnp.int32, (1, lchunk), 1) // bw).astype(F32)
            hit = jnp.zeros((rows, lchunk), F32)
            for idx in picks:
                hit = jnp.where(bid == idx, 1.0, hit)
            sc = jnp.where(hit > 0.0, s_scr[:, cs], NEG)
            s_scr[:, cs] = sc
            m = jnp.maximum(m, jnp.max(sc, axis=-1, keepdims=True))
        po = jnp.exp(so - m)
        l = jnp.sum(po, axis=-1, keepdims=True)
        for cidx in range(nch):
            cs = slice(cidx * lchunk, (cidx + 1) * lchunk)
            pc = jnp.exp(s_scr[:, cs] - m)
            s_scr[:, cs] = pc
            l = l + jnp.sum(pc, axis=-1, keepdims=True)
        inv = 1.0 / l
        for cidx in range(nch):
            cs = slice(cidx * lchunk, (cidx + 1) * lchunk)
            p_ref[0, :, cs] = s_scr[:, cs] * inv
        oown_ref[0] = _dot(po * inv, vn_ref[0])


def _samp_pv_kernel(*refs, npg):
    pt_ref, p_ref, oown_ref = refs[:3]
    vpages = refs[3:3 + npg]
    o_ref, acc_scr = refs[3 + npg:]
    del pt_ref
    g = pl.program_id(1)
    nh, dh, ps = vpages[0].shape[2:]
    rows = p_ref.shape[1]

    @pl.when(g == 0)
    def _init():
        acc_scr[...] = jnp.zeros_like(acc_scr)

    acc = acc_scr[...]
    for j in range(npg):
        acc = acc + _dot_nt(p_ref[0, :, j * ps:(j + 1) * ps], vpages[j][0, 0].reshape(nh * dh, ps))
    acc_scr[...] = acc

    @pl.when(g == pl.num_programs(1) - 1)
    def _finish():
        r8 = lax.broadcasted_iota(jnp.int32, (rows, 1), 0) % nh
        o = oown_ref[0]
        for h in range(nh):
            o = o + jnp.where(r8 == h, acc[:, h * dh:(h + 1) * dh], 0.0)
        o_ref[0] = o


def _attn_sample(qm, kn, vn, cache_k, cache_v, page_table, *, layer):
    bs, rows, dh = qm.shape
    _, _, ps, nh, _ = cache_k.shape
    npages = page_table.shape[1]
    pages_per_blk = MOBA_BLOCK // ps
    assert MOBA_BLOCK % ps == 0 and npages % pages_per_blk == 0 and npages // pages_per_blk <= LANES
    npg = math.gcd(npages, PAGES_PER_STEP)
    total = npages * ps
    ck = jnp.transpose(cache_k, (0, 1, 3, 4, 2))
    cv = jnp.transpose(cache_v, (0, 1, 3, 4, 2))
    head_of_row = jnp.arange(rows, dtype=jnp.int32) % nh
    head_of_col = jnp.arange(nh * dh, dtype=jnp.int32) // dh
    qbd = jnp.where(head_of_row[:, None] == head_of_col[None, :], jnp.tile(qm, (1, 1, nh)), 0.0)

    def page_spec(j):
        return pl.BlockSpec((1, 1, nh, dh, ps), lambda b, g, pt, j=j: (layer, pt[b, g * npg + j], 0, 0, 0))

    row_spec = pl.BlockSpec((1, rows, dh), lambda b, g, pt: (b, 0, 0))
    probs, o_own = pl.pallas_call(
        functools.partial(_samp_scores_kernel, npg=npg, pages_per_blk=pages_per_blk),
        grid_spec=pltpu.PrefetchScalarGridSpec(
            num_scalar_prefetch=1,
            grid=(bs, npages // npg),
            in_specs=[pl.BlockSpec((1, rows, nh * dh), lambda b, g, pt: (b, 0, 0)), row_spec, row_spec, row_spec]
            + [page_spec(j) for j in range(npg)],
            out_specs=[pl.BlockSpec((1, rows, total), lambda b, g, pt: (b, 0, 0)), row_spec],
            scratch_shapes=[pltpu.VMEM((rows, total), F32)],
        ),
        out_shape=[jax.ShapeDtypeStruct((bs, rows, total), F32), jax.ShapeDtypeStruct((bs, rows, dh), F32)],
        compiler_params=_cparams(("arbitrary", "arbitrary")),
        name="moba_sample_scores",
    )(page_table, qbd, qm, kn, vn, *([ck] * npg))
    return pl.pallas_call(
        functools.partial(_samp_pv_kernel, npg=npg),
        grid_spec=pltpu.PrefetchScalarGridSpec(
            num_scalar_prefetch=1,
            grid=(bs, npages // npg),
            in_specs=[pl.BlockSpec((1, rows, npg * ps), lambda b, g, pt: (b, 0, g)), row_spec]
            + [page_spec(j) for j in range(npg)],
            out_specs=row_spec,
            scratch_shapes=[pltpu.VMEM((rows, nh * dh), F32)],
        ),
        out_shape=jax.ShapeDtypeStruct((bs, rows, dh), F32),
        compiler_params=_cparams(("arbitrary", "arbitrary")),
        name="moba_sample_pv",
    )(page_table, probs, o_own, *([cv] * npg))


def _mix_kernel(x_ref, oa_ref, ob_ref, gate_ref, gt_ref, gpost_ref, wa_ref, wb_ref, wo_ref, out_ref):
    d = x_ref.shape[1]
    merged = gate_ref[:, :d] * _dot(oa_ref[...], wa_ref[...]) + gate_ref[:, d:] * _dot(ob_ref[...], wb_ref[...])
    y = _dot(merged, wo_ref[...])
    out_ref[...] = x_ref[...] + gt_ref[0] * _rms(y, gpost_ref[...])


def _mix(x, oa, ob, gates, gt, g_post, wa, wb, wo, tm):
    n, d = x.shape
    nt = n // tm
    nbm, r, _ = gt.shape
    tps = nt // nbm
    resident = dict(pipeline_mode=pl.Buffered(1))

    def rowspec(w):
        return pl.BlockSpec((tm, w), lambda i: (i, 0))

    return pl.pallas_call(
        _mix_kernel,
        grid=(nt,),
        in_specs=[
            rowspec(d), rowspec(oa.shape[1]), rowspec(ob.shape[1]), rowspec(gates.shape[1]),
            pl.BlockSpec((1, r, d), lambda i: (i // tps, 0, 0)),
            pl.BlockSpec((1, d), lambda i: (0, 0)),
            pl.BlockSpec(wa.shape, lambda i: (0, 0), **resident),
            pl.BlockSpec(wb.shape, lambda i: (0, 0), **resident),
            pl.BlockSpec(wo.shape, lambda i: (0, 0), **resident),
        ],
        out_specs=rowspec(d),
        out_shape=jax.ShapeDtypeStruct((n, d), F32),
        compiler_params=_cparams(("arbitrary",)),
        name="merge_out_proj",
    )(x, oa, ob, gates, gt, g_post.reshape(1, d), wa, wb, wo)


def _ffn_kernel(x_ref, sc_ref, sh_ref, gt_ref, gpre_ref, gpost_ref, wgu_ref, wd_ref, out_ref, *, dff, fchunk):
    x = x_ref[...]
    hb = (_rms(x, gpre_ref[...]) * (1.0 + sc_ref[0]) + sh_ref[0]).astype(BF16)
    acc = jnp.zeros(x.shape, F32)
    for j in range(dff // fchunk):
        g = _dot(hb, wgu_ref[:, j * fchunk:(j + 1) * fchunk])
        u = _dot(hb, wgu_ref[:, dff + j * fchunk:dff + (j + 1) * fchunk])
        acc = acc + _dot(g * _sigmoid(g) * u, wd_ref[j * fchunk:(j + 1) * fchunk, :])
    out_ref[...] = x + gt_ref[0] * _rms(acc, gpost_ref[...])


def _ffn(x, sc, sh, gt, g_pre, g_post, wgu, wd, tm):
    n, d = x.shape
    dff = wd.shape[0]
    nt = n // tm
    nbm, r, _ = gt.shape
    tps = nt // nbm
    fchunk = dff // 2 if (dff // 2) % LANES == 0 else dff
    resident = dict(pipeline_mode=pl.Buffered(1))
    modspec = pl.BlockSpec((1, r, d), lambda i: (i // tps, 0, 0))
    vecspec = pl.BlockSpec((1, d), lambda i: (0, 0))
    return pl.pallas_call(
        functools.partial(_ffn_kernel, dff=dff, fchunk=fchunk),
        grid=(nt,),
        in_specs=[
            pl.BlockSpec((tm, d), lambda i: (i, 0)),
            modspec, modspec, modspec, vecspec, vecspec,
            pl.BlockSpec(wgu.shape, lambda i: (0, 0), **resident),
            pl.BlockSpec(wd.shape, lambda i: (0, 0), **resident),
        ],
        out_specs=pl.BlockSpec((tm, d), lambda i: (i, 0)),
        out_shape=jax.ShapeDtypeStruct((n, d), F32),
        compiler_params=_cparams(("arbitrary",)),
        name="swiglu_ffn",
    )(x, sc, sh, gt, g_pre.reshape(1, d), g_post.reshape(1, d), wgu, wd)


def _rope_tables(pos):
    half = AT_DH // 2
    inv = ROPE_THETA ** (-jnp.arange(half, dtype=F32) / half)
    ang = pos.astype(F32)[:, None] * inv[None, :]
    cos = jnp.cos(ang)
    sin = jnp.sin(ang)
    cos_h = jnp.concatenate([cos, cos], axis=-1)
    sin_h = jnp.concatenate([-sin, sin], axis=-1)
    return jnp.tile(cos_h, (1, AT_HEADS)), jnp.tile(sin_h, (1, AT_HEADS))


def _group_layer(x2d, mods, *, bsz, t, pos0, s0, paged, layer, weights, tm, hg_cfg):
    sh1, sc1, gt1, sh2, sc2, gt2 = mods
    (hg_lb, g_pre1, g_post1, w_in_hi, wqk_lo, g_onorm, wa, wb, wo, g_pre2, g_post2, wgu, wd) = weights
    n = x2d.shape[0]
    cos_t, sin_t = _rope_tables(pos0 + jnp.arange(t, dtype=jnp.int32))
    if tm > t:
        cos_t = jnp.tile(cos_t, (tm // t, 1))
        sin_t = jnp.tile(sin_t, (tm // t, 1))
    if paged is None:
        hg4, q, k, kt, vt, gates = _in_proj(x2d, sc1, sh1, g_pre1, cos_t, sin_t, w_in_hi, wqk_lo, tm, seq_len=t)
    else:
        hg4, q, k, v, gates = _in_proj(x2d, sc1, sh1, g_pre1, cos_t, sin_t, w_in_hi, wqk_lo, tm)

    c, bd, rows_per_step = hg_cfg
    tp = -(-t // c) * c
    if tp != t:
        hg4p = jnp.pad(hg4.reshape(bsz, t, -1), ((0, 0), (0, tp - t), (0, 0))).reshape(bsz * tp, -1)
    else:
        hg4p = hg4
    o_a, s_new = _hgrn(hg4p, hg_lb, g_onorm, s0, layer=layer, bsz=bsz, t=tp, c=c, bd=bd,
                       valid=c - (tp - t), rows_per_step=rows_per_step)
    if tp != t:
        o_a = o_a.reshape(bsz, tp, -1)[:, :t].reshape(n, -1)

    if paged is None:
        o_b = _attn_prompt(q, k, vt, bsz=bsz, t=t)
        from_t = lambda a: jnp.transpose(a.reshape(bsz, AT_HEADS, AT_DH, t), (0, 3, 1, 2))
        k_out, v_out = from_t(kt), from_t(vt)
    else:
        cache_k, cache_v, page_table = paged
        to_rows = lambda a: a.reshape(bsz, t * AT_HEADS, AT_DH)
        o_b = _attn_sample(to_rows(q), to_rows(k), to_rows(v), cache_k, cache_v, page_table, layer=layer)
        o_b = o_b.reshape(n, AT_HEADS * AT_DH)
        k_out, v_out = k.reshape(bsz, t, AT_HEADS, AT_DH), v.reshape(bsz, t, AT_HEADS, AT_DH)

    x1 = _mix(x2d, o_a, o_b, gates, gt1, g_post1, wa, wb, wo, tm)
    x2 = _ffn(x1, sc2, sh2, gt2, g_pre2, g_post2, wgu, wd, tm)
    return x2, s_new, k_out, v_out


def kernel(x_prompt, x_sample, c_prompt, c_sample, state_hgrn, cache_k, cache_v, page_table, hg_lb, w_ada, b_ada,
           g_pre1, g_post1, w_in, g_onorm, w_proj_a, w_proj_b, w_out, g_pre2, g_post2, w_gu, w_down):
    bp, t, d = x_prompt.shape
    bs, ts, _ = x_sample.shape
    depth = w_in.shape[0]
    past_len = page_table.shape[1] * cache_k.shape[2]
    assert past_len % MOBA_BLOCK == 0 and ts <= MOBA_BLOCK
    hgw = 2 * HG_HEADS * HG_DK + 2 * HG_HEADS * HG_DV
    atw = AT_HEADS * AT_DH

    xp = x_prompt.reshape(bp * t, d)
    xs = x_sample.reshape(bs * ts, d)
    c_all = jnp.concatenate([c_prompt, c_sample], axis=0)
    outs = [[] for _ in range(6)]
    for l in range(depth):
        mod = _ada_mod(c_all, w_ada[l], b_ada[l])
        mods_p = [m[:bp].reshape(bp, 1, d) for m in jnp.split(mod, 6, axis=-1)]
        mods_s = [jnp.repeat(m[bp:], ts, axis=0).reshape(1, bs * ts, d) for m in jnp.split(mod, 6, axis=-1)]
        _, wqk_lo = _split_weights(w_in[l][:, hgw:hgw + 2 * atw])
        weights = (hg_lb, g_pre1[l], g_post1[l], w_in[l].astype(BF16), wqk_lo, g_onorm[l],
                   w_proj_a[l].astype(BF16), w_proj_b[l].astype(BF16), w_out[l].astype(BF16),
                   g_pre2[l], g_post2[l], w_gu[l].astype(BF16), w_down[l].astype(BF16))
        chunk = HG_CHUNK if t % HG_CHUNK == 0 else t
        xp, sp, kp, vp = _group_layer(xp, mods_p, bsz=bp, t=t, pos0=0, s0=None, paged=None, layer=l,
                                      weights=weights, tm=256, hg_cfg=(chunk, 16, 256))
        cs = -(-ts // SUBLANES) * SUBLANES
        xs, ss, ksn, vsn = _group_layer(xs, mods_s, bsz=bs, t=ts, pos0=past_len, s0=state_hgrn[l],
                                        paged=(cache_k, cache_v, page_table), layer=l,
                                        weights=weights, tm=bs * ts, hg_cfg=(cs, cs, cs))
        for lst, val in zip(outs, (sp, ss, kp, vp, ksn, vsn)):
            lst.append(val)
    return (xp.reshape(bp, t, d), xs.reshape(bs, ts, d)) + tuple(jnp.stack(o) for o in outs)
```

```python
import functools
import math

import jax
import jax.numpy as jnp
from jax import lax
from jax.experimental import pallas as pl
from jax.experimental.pallas import tpu as pltpu

F32 = jnp.float32
BF16 = jnp.bfloat16

HG_HEADS = 4
HG_DK = 128
HG_DV = 128
HG_CHUNK = 64
AT_HEADS = 8
AT_DH = 64
MOBA_BLOCK = 256
MOBA_TOPK = 3
ROPE_THETA = 10000.0
EPS = 1e-6
NEG = -1e30

V7X_VMEM_LIMIT_BYTES = 52 * 1024 * 1024
LANES = 128
SUBLANES = 8
HEADS_PER_VREG = LANES // AT_DH
PAGES_PER_STEP = 16


def _cparams(sem):
    return pltpu.CompilerParams(dimension_semantics=sem, vmem_limit_bytes=V7X_VMEM_LIMIT_BYTES)


def _dot(a, b):
    return jnp.dot(a.astype(BF16), b.astype(BF16), preferred_element_type=F32)


def _dot_nt(a, b):
    return lax.dot_general(a.astype(BF16), b.astype(BF16), (((1,), (1,)), ((), ())), preferred_element_type=F32)


def _dot_tn(a, b):
    return lax.dot_general(a.astype(BF16), b.astype(BF16), (((0,), (0,)), ((), ())), preferred_element_type=F32)


def _split2(a):
    hi = a.astype(BF16)
    lo = (a - hi.astype(F32)).astype(BF16)
    return hi, lo


def _split3(a):
    p1 = a.astype(BF16)
    r1 = a - p1.astype(F32)
    p2 = r1.astype(BF16)
    p3 = (r1 - p2.astype(F32)).astype(BF16)
    return p1, p2, p3


def _dot3(a, b):
    ah, al = _split2(a)
    bh, bl = _split2(b)
    return _dot(ah, bh) + _dot(ah, bl) + _dot(al, bh)


def _dot3_nt(a, b):
    ah, al = _split2(a)
    bh, bl = _split2(b)
    return _dot_nt(ah, bh) + _dot_nt(ah, bl) + _dot_nt(al, bh)


def _sigmoid(x):
    return 1.0 / (1.0 + jnp.exp(-x))


def _rms(x, w):
    return x * lax.rsqrt(jnp.mean(x * x, axis=-1, keepdims=True) + EPS) * w


def _ada_kernel(c_ref, w_ref, b_ref, o_ref):
    c = c_ref[...]
    o_ref[...] = _dot3(c * _sigmoid(c), w_ref[...]) + b_ref[...]


def _ada_mod(c_all, w_ada, b_ada):
    rows, d = c_all.shape
    n = w_ada.shape[1]
    tn = 1024
    return pl.pallas_call(
        _ada_kernel,
        grid=(n // tn,),
        in_specs=[
            pl.BlockSpec((rows, d), lambda j: (0, 0)),
            pl.BlockSpec((d, tn), lambda j: (0, j)),
            pl.BlockSpec((1, tn), lambda j: (0, j)),
        ],
        out_specs=pl.BlockSpec((rows, tn), lambda j: (0, j)),
        out_shape=jax.ShapeDtypeStruct((rows, n), F32),
        compiler_params=_cparams(("arbitrary",)),
        name="ada_mod",
    )(c_all, w_ada, b_ada.reshape(1, n))


def _split_kernel(w_ref, hi_ref, lo_ref):
    hi, lo = _split2(w_ref[...])
    hi_ref[...] = hi
    lo_ref[...] = lo


def _split_weights(w):
    r, c = w.shape
    tr = 256
    return pl.pallas_call(
        _split_kernel,
        grid=(r // tr,),
        in_specs=[pl.BlockSpec((tr, c), lambda i: (i, 0))],
        out_specs=[pl.BlockSpec((tr, c), lambda i: (i, 0))] * 2,
        out_shape=[jax.ShapeDtypeStruct((r, c), BF16)] * 2,
        compiler_params=_cparams(("arbitrary",)),
        name="split_weights",
    )(w)


def _rope_group(y, cos, sin_signed):
    lane = lax.broadcasted_iota(jnp.int32, (1, LANES), 1)
    first_half = (lane % AT_DH) < (AT_DH // 2)
    rot = jnp.where(first_half, pltpu.roll(y, LANES - AT_DH // 2, 1), pltpu.roll(y, AT_DH // 2, 1))
    return y * cos + rot * sin_signed


def _in_proj_kernel(x_ref, sc_ref, sh_ref, g_ref, cos_ref, sin_ref, w_ref, wlo_ref, *outs, hgw, atw, kv_t):
    if kv_t:
        hg_ref, q_ref, k_ref, kt_ref, vt_ref, gate_ref = outs
    else:
        hg_ref, q_ref, k_ref, v_ref, gate_ref = outs
    x = x_ref[...]
    h = _rms(x, g_ref[...]) * (1.0 + sc_ref[0]) + sh_ref[0]
    hh, hl = _split2(h)
    seg = 512
    for j in range(hgw // seg):
        hg_ref[:, j * seg:(j + 1) * seg] = _dot(hh, w_ref[:, j * seg:(j + 1) * seg])
    tm = x.shape[0]
    hs = jnp.concatenate([hh, hl], axis=0)
    for idx, out in enumerate((q_ref, k_ref)):
        c0 = hgw + idx * atw
        a = jnp.dot(hs, w_ref[:, c0:c0 + atw], preferred_element_type=F32)
        y = a[:tm] + a[tm:] + _dot(hh, wlo_ref[:, idx * atw:(idx + 1) * atw])
        for g in range(atw // LANES):
            yg = _rope_group(y[:, g * LANES:(g + 1) * LANES], cos_ref[...], sin_ref[...])
            out[:, g * LANES:(g + 1) * LANES] = yg
            if kv_t and idx == 1:
                kt_ref[0, g * LANES:(g + 1) * LANES, :] = yg.T
    c0 = hgw + 2 * atw
    v = _dot(hh, w_ref[:, c0:c0 + atw])
    if kv_t:
        for g in range(atw // LANES):
            vt_ref[0, g * LANES:(g + 1) * LANES, :] = v[:, g * LANES:(g + 1) * LANES].T
    else:
        v_ref[...] = v
    c0 = hgw + 3 * atw
    gw = gate_ref.shape[1]
    for j in range(gw // seg):
        gate = _sigmoid(_dot(hh, w_ref[:, c0 + j * seg:c0 + (j + 1) * seg]))
        gate_ref[:, j * seg:(j + 1) * seg] = gate.astype(gate_ref.dtype)


def _in_proj(x, sc, sh, g_pre, cos_t, sin_t, w_hi, wqk_lo, tm, seq_len=None):
    n, d = x.shape
    kv_t = seq_len is not None
    hgw = 2 * HG_HEADS * HG_DK + 2 * HG_HEADS * HG_DV
    atw = AT_HEADS * AT_DH
    gw = 2 * d
    ncols = w_hi.shape[1]
    assert ncols == hgw + 3 * atw + gw
    nt = n // tm
    nbm, r, _ = sc.shape
    tps = nt // nbm
    ctiles = cos_t.shape[0] // tm
    resident = dict(pipeline_mode=pl.Buffered(1))
    rowspec = lambda w: pl.BlockSpec((tm, w), lambda i: (i, 0))
    rowshape = lambda w: jax.ShapeDtypeStruct((n, w), F32)
    if kv_t:
        tiles_per_seq = seq_len // tm
        tspec = pl.BlockSpec((1, atw, tm), lambda i: (i // tiles_per_seq, 0, i % tiles_per_seq))
        tshape = jax.ShapeDtypeStruct((n // seq_len, atw, seq_len), F32)
        kv_specs, kv_shapes = [rowspec(atw), tspec, tspec], [rowshape(atw), tshape, tshape]
    else:
        kv_specs, kv_shapes = [rowspec(atw), rowspec(atw)], [rowshape(atw), rowshape(atw)]
    return pl.pallas_call(
        functools.partial(_in_proj_kernel, hgw=hgw, atw=atw, kv_t=kv_t),
        grid=(nt,),
        in_specs=[
            pl.BlockSpec((tm, d), lambda i: (i, 0)),
            pl.BlockSpec((1, r, d), lambda i: (i // tps, 0, 0)),
            pl.BlockSpec((1, r, d), lambda i: (i // tps, 0, 0)),
            pl.BlockSpec((1, d), lambda i: (0, 0)),
            pl.BlockSpec((tm, LANES), lambda i: (i % ctiles, 0)),
            pl.BlockSpec((tm, LANES), lambda i: (i % ctiles, 0)),
            pl.BlockSpec((d, ncols), lambda i: (0, 0), **resident),
            pl.BlockSpec((d, 2 * atw), lambda i: (0, 0), **resident),
        ],
        out_specs=[rowspec(hgw), rowspec(atw)] + kv_specs + [rowspec(gw)],
        out_shape=[rowshape(hgw), rowshape(atw)] + kv_shapes + [jax.ShapeDtypeStruct((n, gw), BF16)],
        compiler_params=_cparams(("arbitrary",)),
        name="in_proj",
    )(x, sc, sh, g_pre.reshape(1, d), cos_t, sin_t, w_hi, wqk_lo)


def _cumsum_rows(g):
    c = g.shape[0]
    row = lax.broadcasted_iota(jnp.int32, (c, c), 0)
    col = lax.broadcasted_iota(jnp.int32, (c, c), 1)
    tri = jnp.where(row >= col, 1.0, 0.0).astype(BF16)
    p1, p2, p3 = _split3(g)
    return (jnp.dot(tri, p1, preferred_element_type=F32) + jnp.dot(tri, p2, preferred_element_type=F32)
            + jnp.dot(tri, p3, preferred_element_type=F32))


def _hgrn_chunk(q, hf, v, hg, lb, gon, st, *, c, bd, valid):
    f = lb + (1.0 - lb) * _sigmoid(hf)
    gl = jnp.log(f)
    kin = 1.0 - f
    row = lax.broadcasted_iota(jnp.int32, (c, 1), 0)
    if valid < c:
        live = row < valid
        gl = jnp.where(live, gl, 0.0)
        kin = jnp.where(live, kin, 0.0)
    b = _cumsum_rows(gl)
    o = _dot_nt(q * jnp.exp(b), st)
    a = None
    m = c // 2
    while m >= bd:
        span = 2 * m
        right = (row % span) >= m
        npar = c // span
        ref = b[m - 1:m]
        for p in range(1, npar):
            ref = jnp.where(row // span == p, b[p * span + m - 1:p * span + m], ref)
        eq = jnp.exp(jnp.where(right, b - ref, NEG))
        ek = jnp.exp(jnp.where(right, NEG, ref - b))
        al = _dot_nt(q * eq, kin * ek)
        if npar > 1:
            rp = lax.broadcasted_iota(jnp.int32, (c, c), 0) // span
            cp = lax.broadcasted_iota(jnp.int32, (c, c), 1) // span
            al = jnp.where(rp == cp, al, 0.0)
        a = al if a is None else a + al
        m //= 2
    if a is not None:
        o = o + _dot(a, v)
    trow = lax.broadcasted_iota(jnp.int32, (bd, 1), 0)
    blocks = []
    for i in range(c // bd):
        sl = slice(i * bd, (i + 1) * bd)
        bi, qi, ki, vi = b[sl], q[sl], kin[sl], v[sl]
        od = jnp.zeros((bd, v.shape[1]), F32)
        for s in range(bd):
            e = jnp.exp(jnp.where(trow >= s, bi - bi[s:s + 1], NEG))
            ac = jnp.sum(qi * e * ki[s:s + 1], axis=-1, keepdims=True)
            od = od + ac * vi[s:s + 1]
        blocks.append(od)
    o = o + (blocks[0] if len(blocks) == 1 else jnp.concatenate(blocks, axis=0))
    bl = b[c - 1:c]
    st_new = st * jnp.exp(bl) + _dot_tn(v, kin * jnp.exp(bl - b))
    y = _rms(o, gon) * (hg * _sigmoid(hg))
    return y, st_new


def _hgrn_kernel(*refs, layer, c, bd, valid, n_inner, use_s0):
    if use_s0:
        lb_ref, gon_ref, q_ref, f_ref, i_ref, g_ref, s0_ref, o_ref, sout_ref, st_scr = refs
    else:
        lb_ref, gon_ref, q_ref, f_ref, i_ref, g_ref, o_ref, sout_ref, st_scr = refs
    cb = pl.program_id(2)

    @pl.when(cb == 0)
    def _init():
        if use_s0:
            st_scr[...] = s0_ref[0, 0].T
        else:
            st_scr[...] = jnp.zeros_like(st_scr)

    rows = [lb_ref[j:j + 1, :] for j in range(lb_ref.shape[0])]
    mx = rows[0]
    for r in rows[1:]:
        mx = jnp.maximum(mx, r)
    es = [jnp.exp(r - mx) for r in rows]
    tot = es[0]
    for e in es[1:]:
        tot = tot + e
    part = es[0]
    for e in es[1:layer + 1]:
        part = part + e
    lb = part / tot
    gon = gon_ref[...]

    st = st_scr[...]
    for j in range(n_inner):
        sl = slice(j * c, (j + 1) * c)
        y, st = _hgrn_chunk(q_ref[sl, :], f_ref[sl, :], i_ref[sl, :], g_ref[sl, :], lb, gon, st,
                            c=c, bd=bd, valid=valid)
        o_ref[sl, :] = y.astype(o_ref.dtype)
    st_scr[...] = st

    @pl.when(cb == pl.num_programs(2) - 1)
    def _fin():
        sout_ref[0, 0] = st_scr[...].T


def _hgrn(hg4, hg_lb, g_onorm, s0, *, layer, bsz, t, c, bd, valid, rows_per_step, out_dtype):
    n = hg4.shape[0]
    h = HG_HEADS
    n_inner = rows_per_step // c
    steps = t // rows_per_step
    use_s0 = s0 is not None

    def col(k):
        return pl.BlockSpec((rows_per_step, HG_DK), lambda b, hh, cb, k=k: (b * steps + cb, k * h + hh))

    in_specs = [
        pl.BlockSpec((hg_lb.shape[0], HG_DK), lambda b, hh, cb: (0, hh)),
        pl.BlockSpec((1, HG_DV), lambda b, hh, cb: (0, 0)),
        col(0), col(1), col(2), col(3),
    ]
    args = [hg_lb, g_onorm.reshape(1, HG_DV), hg4, hg4, hg4, hg4]
    if use_s0:
        in_specs.append(pl.BlockSpec((1, 1, HG_DK, HG_DV), lambda b, hh, cb: (b, hh, 0, 0)))
        args.append(s0)
    return pl.pallas_call(
        functools.partial(_hgrn_kernel, layer=layer, c=c, bd=bd, valid=valid, n_inner=n_inner, use_s0=use_s0),
        grid=(bsz, h, steps),
        in_specs=in_specs,
        out_specs=[
            pl.BlockSpec((rows_per_step, HG_DV), lambda b, hh, cb: (b * steps + cb, hh)),
            pl.BlockSpec((1, 1, HG_DK, HG_DV), lambda b, hh, cb: (b, hh, 0, 0)),
        ],
        out_shape=[
            jax.ShapeDtypeStruct((n, h * HG_DV), out_dtype),
            jax.ShapeDtypeStruct((bsz, h, HG_DK, HG_DV), F32),
        ],
        scratch_shapes=[pltpu.VMEM((HG_DV, HG_DK), F32)],
        compiler_params=_cparams(("arbitrary", "arbitrary", "arbitrary")),
        name="hgrn2",
    )(*args)


def _topk_select_t(g, own, nblk):
    rown = lax.broadcasted_iota(jnp.int32, (g.shape[0], 1), 0)
    past = rown < own
    gm = jnp.where(past, g, NEG)
    rank = jnp.zeros(g.shape, F32)
    for m in range(nblk - 1):
        grow = gm[m:m + 1, :]
        tie = jnp.where(rown > m, 1.0, 0.0)
        rank = rank + jnp.where(grow > gm, 1.0, 0.0) + jnp.where(grow == gm, tie, 0.0)
    return jnp.where(past, jnp.where(rank < MOBA_TOPK, 1.0, 0.0), 0.0)


def _attn_kernel(q_ref, k_ref, vt_ref, o_ref, km_scr, qh_scr, sel_scr, s_scr, acc_scr, *, nblk):
    tq = MOBA_BLOCK
    own = pl.program_id(1)
    scale = AT_DH ** -0.5
    ngrp = AT_HEADS // HEADS_PER_VREG

    @pl.when(own == 0)
    def _means():
        km_scr[...] = jnp.zeros_like(km_scr)
        for n in range(nblk):
            blk = k_ref[n * MOBA_BLOCK:(n + 1) * MOBA_BLOCK, :]
            km_scr[n:n + 1, :] = jnp.sum(blk, axis=0, keepdims=True) * (1.0 / MOBA_BLOCK)

    lane = lax.broadcasted_iota(jnp.int32, (1, LANES), 1)
    rown = lax.broadcasted_iota(jnp.int32, (km_scr.shape[0], 1), 0)
    causal = (lax.broadcasted_iota(jnp.int32, (MOBA_BLOCK, tq), 0)
              <= lax.broadcasted_iota(jnp.int32, (MOBA_BLOCK, tq), 1))
    r_own = pl.multiple_of(own * MOBA_BLOCK, MOBA_BLOCK)

    ms = []
    for grp in range(ngrp):
        lo = grp * LANES
        qp = q_ref[:, lo:lo + LANES]
        kmp = km_scr[:, lo:lo + LANES]
        kb = k_ref[pl.ds(r_own, MOBA_BLOCK), lo:lo + LANES].astype(BF16)
        for sub in range(HEADS_PER_VREG):
            h = grp * HEADS_PER_VREG + sub
            qh = jnp.where((lane // AT_DH) == sub, qp, 0.0) * scale
            sel_scr[h] = _topk_select_t(_dot3_nt(kmp, qh), own, nblk)
            qhb = qh.astype(BF16)
            qh_scr[h] = qhb
            st = jnp.where(causal, _dot_nt(kb, qhb), NEG)
            s_scr[h, own] = st
            ms.append(jnp.max(st, axis=0, keepdims=True))

    def pass_a(n, ms):
        rows = pl.ds(pl.multiple_of(n * MOBA_BLOCK, MOBA_BLOCK), MOBA_BLOCK)
        out = []
        for grp in range(ngrp):
            kb = k_ref[rows, grp * LANES:(grp + 1) * LANES].astype(BF16)
            for sub in range(HEADS_PER_VREG):
                h = grp * HEADS_PER_VREG + sub
                selrow = jnp.max(jnp.where(rown == n, sel_scr[h], 0.0), axis=0, keepdims=True)
                st = jnp.where(selrow > 0.0, _dot_nt(kb, qh_scr[h]), NEG)
                s_scr[h, n] = st
                out.append(jnp.maximum(ms[h], jnp.max(st, axis=0, keepdims=True)))
        return tuple(out)

    ms = lax.fori_loop(0, own, pass_a, tuple(ms))

    acc_scr[...] = jnp.zeros_like(acc_scr)

    def pass_b(n, ls):
        cols = pl.ds(pl.multiple_of(n * MOBA_BLOCK, MOBA_BLOCK), MOBA_BLOCK)
        out = []
        for h in range(AT_HEADS):
            p = jnp.exp(s_scr[h, n] - ms[h])
            acc_scr[h] = acc_scr[h] + _dot(vt_ref[0, h * AT_DH:(h + 1) * AT_DH, cols], p)
            out.append(ls[h] + jnp.sum(p, axis=0, keepdims=True))
        return tuple(out)

    ls = lax.fori_loop(0, own + 1, pass_b, tuple(jnp.zeros((1, tq), F32) for _ in range(AT_HEADS)))
    for grp in range(ngrp):
        heads = range(grp * HEADS_PER_VREG, (grp + 1) * HEADS_PER_VREG)
        ot = jnp.concatenate([acc_scr[h] / ls[h] for h in heads], axis=0)
        o_ref[:, grp * LANES:(grp + 1) * LANES] = ot.T.astype(o_ref.dtype)


def _attn_prompt(q, k, vt, *, bsz, t):
    n, w = q.shape
    nblk = t // MOBA_BLOCK
    assert t % MOBA_BLOCK == 0
    rpad = -(-nblk // SUBLANES) * SUBLANES
    return pl.pallas_call(
        functools.partial(_attn_kernel, nblk=nblk),
        grid=(bsz, nblk),
        in_specs=[
            pl.BlockSpec((MOBA_BLOCK, w), lambda b, i: (b * nblk + i, 0)),
            pl.BlockSpec((t, w), lambda b, i: (b, 0)),
            pl.BlockSpec((1, w, t), lambda b, i: (b, 0, 0)),
        ],
        out_specs=pl.BlockSpec((MOBA_BLOCK, w), lambda b, i: (b * nblk + i, 0)),
        out_shape=jax.ShapeDtypeStruct((n, w), BF16),
        scratch_shapes=[pltpu.VMEM((rpad, w), F32),
                        pltpu.VMEM((AT_HEADS, MOBA_BLOCK, LANES), BF16),
                        pltpu.VMEM((AT_HEADS, rpad, MOBA_BLOCK), F32),
                        pltpu.VMEM((AT_HEADS, nblk, MOBA_BLOCK, MOBA_BLOCK), F32),
                        pltpu.VMEM((AT_HEADS, AT_DH, MOBA_BLOCK), F32)],
        compiler_params=_cparams(("arbitrary", "arbitrary")),
        name="moba_prompt",
    )(q, k, vt)


def _samp_scores_kernel(*refs, npg, pages_per_blk):
    pt_ref, qbd_ref, qm_ref, kn_ref, vn_ref = refs[:5]
    kpages = refs[5:5 + npg]
    p_ref, oown_ref, s_scr = refs[5 + npg:]
    del pt_ref
    g = pl.program_id(1)
    nh, dh, ps = kpages[0].shape[2:]
    rows = qm_ref.shape[1]
    scale = dh ** -0.5

    qh, ql = _split2(qbd_ref[0] * scale)
    qs = jnp.concatenate([qh, ql], axis=0)
    grp = 2 if npg % 2 == 0 else 1
    for j in range(0, npg, grp):
        kp = jnp.concatenate([kpages[j + i][0, 0].reshape(nh * dh, ps) for i in range(grp)], axis=1)
        kh, kl = _split2(kp)
        a = jnp.dot(qs, kh, preferred_element_type=F32)
        s = a[:rows] + a[rows:] + jnp.dot(qh, kl, preferred_element_type=F32)
        s_scr[:, pl.ds(pl.multiple_of((g * npg + j) * ps, ps), grp * ps)] = s

    @pl.when(g == pl.num_programs(1) - 1)
    def _finish():
        total = s_scr.shape[1]
        bw = pages_per_blk * ps
        nblk = total // bw
        lane = lax.broadcasted_iota(jnp.int32, (1, LANES), 1)
        gate = jnp.full((rows, LANES), NEG, F32)
        for n in range(nblk):
            col = jnp.sum(s_scr[:, n * bw:(n + 1) * bw], axis=-1, keepdims=True) * (1.0 / bw)
            gate = jnp.where(lane == n, col, gate)
        lanef = lane.astype(F32)
        cur = gate
        picks = []
        for _ in range(min(MOBA_TOPK, nblk)):
            mx = jnp.max(cur, axis=-1, keepdims=True)
            idx = jnp.min(jnp.where(cur == mx, lanef, float(LANES)), axis=-1, keepdims=True)
            picks.append(idx)
            cur = jnp.where(lanef == idx, NEG, cur)
        qm = qm_ref[0] * scale
        so = _dot_nt(qm, kn_ref[0])
        ro = lax.broadcasted_iota(jnp.int32, (rows, 1), 0)
        co = lax.broadcasted_iota(jnp.int32, (1, rows), 1)
        own_ok = jnp.logical_and(ro % nh == co % nh, co // nh <= ro // nh)
        so = jnp.where(own_ok, so, NEG)
        m = jnp.max(so, axis=-1, keepdims=True)
        lchunk = 8 * bw if total % (8 * bw) == 0 else bw
        nch = total // lchunk
        for cidx in range(nch):
            cs = slice(cidx * lchunk, (cidx + 1) * lchunk)
            bid = (cidx * (lchunk // bw) + lax.broadcasted_iota(jnp.int32, (1, lchunk), 1) // bw).astype(F32)
            hit = jnp.zeros((rows, lchunk), F32)
            for idx in picks:
                hit = jnp.where(bid == idx, 1.0, hit)
            sc = jnp.where(hit > 0.0, s_scr[:, cs], NEG)
            s_scr[:, cs] = sc
            m = jnp.maximum(m, jnp.max(sc, axis=-1, keepdims=True))
        po = jnp.exp(so - m)
        l = jnp.sum(po, axis=-1, keepdims=True)
        for cidx in range(nch):
            cs = slice(cidx * lchunk, (cidx + 1) * lchunk)
            pc = jnp.exp(s_scr[:, cs] - m)
            s_scr[:, cs] = pc
            l = l + jnp.sum(pc, axis=-1, keepdims=True)
        inv = 1.0 / l
        for cidx in range(nch):
            cs = slice(cidx * lchunk, (cidx + 1) * lchunk)
            p_ref[0, :, cs] = s_scr[:, cs] * inv
        oown_ref[0] = _dot(po * inv, vn_ref[0])


def _samp_pv_kernel(*refs, npg):
    pt_ref, p_ref, oown_ref = refs[:3]
    vpages = refs[3:3 + npg]
    o_ref, acc_scr = refs[3 + npg:]
    del pt_ref
    g = pl.program_id(1)
    nh, dh, ps = vpages[0].shape[2:]
    rows = p_ref.shape[1]

    @pl.when(g == 0)
    def _init():
        acc_scr[...] = jnp.zeros_like(acc_scr)

    acc = acc_scr[...]
    grp = 2 if npg % 2 == 0 else 1
    for j in range(0, npg, grp):
        vp = jnp.concatenate([vpages[j + i][0, 0].reshape(nh * dh, ps) for i in range(grp)], axis=1)
        acc = acc + _dot_nt(p_ref[0, :, j * ps:(j + grp) * ps], vp)
    acc_scr[...] = acc

    @pl.when(g == pl.num_programs(1) - 1)
    def _finish():
        r8 = lax.broadcasted_iota(jnp.int32, (rows, 1), 0) % nh
        o = oown_ref[0]
        for h in range(nh):
            o = o + jnp.where(r8 == h, acc[:, h * dh:(h + 1) * dh], 0.0)
        o_ref[0] = o


def _attn_sample(qm, kn, vn, cache_k, cache_v, page_table, *, layer):
    bs, rows, dh = qm.shape
    _, _, ps, nh, _ = cache_k.shape
    npages = page_table.shape[1]
    pages_per_blk = MOBA_BLOCK // ps
    assert MOBA_BLOCK % ps == 0 and npages % pages_per_blk == 0 and npages // pages_per_blk <= LANES
    npg = math.gcd(npages, PAGES_PER_STEP)
    total = npages * ps
    ck = jnp.transpose(cache_k, (0, 1, 3, 4, 2))
    cv = jnp.transpose(cache_v, (0, 1, 3, 4, 2))
    head_of_row = jnp.arange(rows, dtype=jnp.int32) % nh
    head_of_col = jnp.arange(nh * dh, dtype=jnp.int32) // dh
    qbd = jnp.where(head_of_row[:, None] == head_of_col[None, :], jnp.tile(qm, (1, 1, nh)), 0.0)

    def page_spec(j):
        return pl.BlockSpec((1, 1, nh, dh, ps), lambda b, g, pt, j=j: (layer, pt[b, g * npg + j], 0, 0, 0))

    row_spec = pl.BlockSpec((1, rows, dh), lambda b, g, pt: (b, 0, 0))
    probs, o_own = pl.pallas_call(
        functools.partial(_samp_scores_kernel, npg=npg, pages_per_blk=pages_per_blk),
        grid_spec=pltpu.PrefetchScalarGridSpec(
            num_scalar_prefetch=1,
            grid=(bs, npages // npg),
            in_specs=[pl.BlockSpec((1, rows, nh * dh), lambda b, g, pt: (b, 0, 0)), row_spec, row_spec, row_spec]
            + [page_spec(j) for j in range(npg)],
            out_specs=[pl.BlockSpec((1, rows, total), lambda b, g, pt: (b, 0, 0)), row_spec],
            scratch_shapes=[pltpu.VMEM((rows, total), F32)],
        ),
        out_shape=[jax.ShapeDtypeStruct((bs, rows, total), F32), jax.ShapeDtypeStruct((bs, rows, dh), F32)],
        compiler_params=_cparams(("arbitrary", "arbitrary")),
        name="moba_sample_scores",
    )(page_table, qbd, qm, kn, vn, *([ck] * npg))
    return pl.pallas_call(
        functools.partial(_samp_pv_kernel, npg=npg),
        grid_spec=pltpu.PrefetchScalarGridSpec(
            num_scalar_prefetch=1,
            grid=(bs, npages // npg),
            in_specs=[pl.BlockSpec((1, rows, npg * ps), lambda b, g, pt: (b, 0, g)), row_spec]
            + [page_spec(j) for j in range(npg)],
            out_specs=row_spec,
            scratch_shapes=[pltpu.VMEM((rows, nh * dh), F32)],
        ),
        out_shape=jax.ShapeDtypeStruct((bs, rows, dh), F32),
        compiler_params=_cparams(("arbitrary", "arbitrary")),
        name="moba_sample_pv",
    )(page_table, probs, o_own, *([cv] * npg))


def _mix_ffn_kernel(x_ref, oa_ref, ob_ref, gate_ref, gt1_ref, sc2_ref, sh2_ref, gt2_ref,
                    gpost1_ref, gpre2_ref, gpost2_ref, wa_ref, wb_ref, wo_ref, wgu_ref, wd_ref, out_ref,
                    *, dff, fchunk):
    d = x_ref.shape[1]
    merged = gate_ref[:, :d] * _dot(oa_ref[...], wa_ref[...]) + gate_ref[:, d:] * _dot(ob_ref[...], wb_ref[...])
    x1 = x_ref[...] + gt1_ref[0] * _rms(_dot(merged, wo_ref[...]), gpost1_ref[...])
    hb = (_rms(x1, gpre2_ref[...]) * (1.0 + sc2_ref[0]) + sh2_ref[0]).astype(BF16)
    acc = jnp.zeros(x1.shape, F32)
    for j in range(dff // fchunk):
        g = _dot(hb, wgu_ref[:, j * fchunk:(j + 1) * fchunk])
        u = _dot(hb, wgu_ref[:, dff + j * fchunk:dff + (j + 1) * fchunk])
        acc = acc + _dot(g * _sigmoid(g) * u, wd_ref[j * fchunk:(j + 1) * fchunk, :])
    out_ref[...] = x1 + gt2_ref[0] * _rms(acc, gpost2_ref[...])


def _mix_ffn(x, oa, ob, gates, gt1, sc2, sh2, gt2, g_post1, g_pre2, g_post2, wa, wb, wo, wgu, wd, tm):
    n, d = x.shape
    dff = wd.shape[0]
    nt = n // tm
    nbm, r, _ = gt1.shape
    tps = nt // nbm
    fchunk = dff // 2 if (dff // 2) % LANES == 0 else dff
    resident = dict(pipeline_mode=pl.Buffered(1))
    rowspec = lambda w: pl.BlockSpec((tm, w), lambda i: (i, 0))
    modspec = pl.BlockSpec((1, r, d), lambda i: (i // tps, 0, 0))
    vecspec = pl.BlockSpec((1, d), lambda i: (0, 0))
    wspec = lambda w: pl.BlockSpec(w.shape, lambda i: (0, 0), **resident)
    return pl.pallas_call(
        functools.partial(_mix_ffn_kernel, dff=dff, fchunk=fchunk),
        grid=(nt,),
        in_specs=[rowspec(d), rowspec(oa.shape[1]), rowspec(ob.shape[1]), rowspec(gates.shape[1]),
                  modspec, modspec, modspec, modspec, vecspec, vecspec, vecspec,
                  wspec(wa), wspec(wb), wspec(wo), wspec(wgu), wspec(wd)],
        out_specs=rowspec(d),
        out_shape=jax.ShapeDtypeStruct((n, d), F32),
        compiler_params=_cparams(("arbitrary",)),
        name="merge_proj_ffn",
    )(x, oa, ob, gates, gt1, sc2, sh2, gt2, g_post1.reshape(1, d), g_pre2.reshape(1, d), g_post2.reshape(1, d),
      wa, wb, wo, wgu, wd)


def _rope_tables(pos):
    half = AT_DH // 2
    inv = ROPE_THETA ** (-jnp.arange(half, dtype=F32) / half)
    ang = pos.astype(F32)[:, None] * inv[None, :]
    cos = jnp.cos(ang)
    sin = jnp.sin(ang)
    cos_h = jnp.concatenate([cos, cos], axis=-1)
    sin_h = jnp.concatenate([-sin, sin], axis=-1)
    return jnp.tile(cos_h, (1, HEADS_PER_VREG)), jnp.tile(sin_h, (1, HEADS_PER_VREG))


def _group_layer(x2d, mods, *, bsz, t, pos0, s0, paged, layer, weights, tm, hg_cfg):
    sh1, sc1, gt1, sh2, sc2, gt2 = mods
    (hg_lb, g_pre1, g_post1, w_in_hi, wqk_lo, g_onorm, wa, wb, wo, g_pre2, g_post2, wgu, wd) = weights
    n = x2d.shape[0]
    cos_t, sin_t = _rope_tables(pos0 + jnp.arange(t, dtype=jnp.int32))
    if tm > t:
        cos_t = jnp.tile(cos_t, (tm // t, 1))
        sin_t = jnp.tile(sin_t, (tm // t, 1))
    if paged is None:
        hg4, q, k, kt, vt, gates = _in_proj(x2d, sc1, sh1, g_pre1, cos_t, sin_t, w_in_hi, wqk_lo, tm, seq_len=t)
    else:
        hg4, q, k, v, gates = _in_proj(x2d, sc1, sh1, g_pre1, cos_t, sin_t, w_in_hi, wqk_lo, tm)

    c, bd, rows_per_step = hg_cfg
    tp = -(-t // c) * c
    if tp != t:
        hg4p = jnp.pad(hg4.reshape(bsz, t, -1), ((0, 0), (0, tp - t), (0, 0))).reshape(bsz * tp, -1)
    else:
        hg4p = hg4
    o_a, s_new = _hgrn(hg4p, hg_lb, g_onorm, s0, layer=layer, bsz=bsz, t=tp, c=c, bd=bd,
                       valid=c - (tp - t), rows_per_step=rows_per_step,
                       out_dtype=BF16 if rows_per_step % (2 * SUBLANES) == 0 else F32)
    if tp != t:
        o_a = o_a.reshape(bsz, tp, -1)[:, :t].reshape(n, -1)

    if paged is None:
        o_b = _attn_prompt(q, k, vt, bsz=bsz, t=t)
        from_t = lambda a: jnp.transpose(a.reshape(bsz, AT_HEADS, AT_DH, t), (0, 3, 1, 2))
        k_out, v_out = from_t(kt), from_t(vt)
    else:
        cache_k, cache_v, page_table = paged
        to_rows = lambda a: a.reshape(bsz, t * AT_HEADS, AT_DH)
        o_b = _attn_sample(to_rows(q), to_rows(k), to_rows(v), cache_k, cache_v, page_table, layer=layer)
        o_b = o_b.reshape(n, AT_HEADS * AT_DH)
        k_out, v_out = k.reshape(bsz, t, AT_HEADS, AT_DH), v.reshape(bsz, t, AT_HEADS, AT_DH)

    x2 = _mix_ffn(x2d, o_a, o_b, gates, gt1, sc2, sh2, gt2, g_post1, g_pre2, g_post2, wa, wb, wo, wgu, wd, tm)
    return x2, s_new, k_out, v_out


def kernel(x_prompt, x_sample, c_prompt, c_sample, state_hgrn, cache_k, cache_v, page_table, hg_lb, w_ada, b_ada,
           g_pre1, g_post1, w_in, g_onorm, w_proj_a, w_proj_b, w_out, g_pre2, g_post2, w_gu, w_down):
    bp, t, d = x_prompt.shape
    bs, ts, _ = x_sample.shape
    depth = w_in.shape[0]
    past_len = page_table.shape[1] * cache_k.shape[2]
    assert past_len % MOBA_BLOCK == 0 and ts <= MOBA_BLOCK
    hgw = 2 * HG_HEADS * HG_DK + 2 * HG_HEADS * HG_DV
    atw = AT_HEADS * AT_DH

    xp = x_prompt.reshape(bp * t, d)
    xs = x_sample.reshape(bs * ts, d)
    c_all = jnp.concatenate([c_prompt, c_sample], axis=0)
    outs = [[] for _ in range(6)]
    for l in range(depth):
        mod = _ada_mod(c_all, w_ada[l], b_ada[l])
        mods_p = [m[:bp].reshape(bp, 1, d) for m in jnp.split(mod, 6, axis=-1)]
        mods_s = [jnp.repeat(m[bp:], ts, axis=0).reshape(1, bs * ts, d) for m in jnp.split(mod, 6, axis=-1)]
        _, wqk_lo = _split_weights(w_in[l][:, hgw:hgw + 2 * atw])
        weights = (hg_lb, g_pre1[l], g_post1[l], w_in[l].astype(BF16), wqk_lo, g_onorm[l],
                   w_proj_a[l].astype(BF16), w_proj_b[l].astype(BF16), w_out[l].astype(BF16),
                   g_pre2[l], g_post2[l], w_gu[l].astype(BF16), w_down[l].astype(BF16))
        chunk = HG_CHUNK if t % HG_CHUNK == 0 else t
        xp, sp, kp, vp = _group_layer(xp, mods_p, bsz=bp, t=t, pos0=0, s0=None, paged=None, layer=l,
                                      weights=weights, tm=512, hg_cfg=(chunk, 8, 512))
        cs = -(-ts // SUBLANES) * SUBLANES
        xs, ss, ksn, vsn = _group_layer(xs, mods_s, bsz=bs, t=ts, pos0=past_len, s0=state_hgrn[l],
                                        paged=(cache_k, cache_v, page_table), layer=l,
                                        weights=weights, tm=bs * ts, hg_cfg=(cs, cs, cs))
        for lst, val in zip(outs, (sp, ss, kp, vp, ksn, vsn)):
            lst.append(val)
    return (xp.reshape(bp, t, d), xs.reshape(bs, ts, d)) + tuple(jnp.stack(o) for o in outs)
```

```python
import functools
import math

import jax
import jax.numpy as jnp
from jax import lax
from jax.experimental import pallas as pl
from jax.experimental.pallas import tpu as pltpu

F32 = jnp.float32
BF16 = jnp.bfloat16

HG_HEADS = 4
HG_DK = 128
HG_DV = 128
HG_CHUNK = 64
AT_HEADS = 8
AT_DH = 64
MOBA_BLOCK = 256
MOBA_TOPK = 3
ROPE_THETA = 10000.0
EPS = 1e-6
NEG = -1e30
LOG2E = math.log2(math.e)

V7X_VMEM_LIMIT_BYTES = 52 * 1024 * 1024
LANES = 128
SUBLANES = 8
HEADS_PER_VREG = LANES // AT_DH
PAGES_PER_STEP = 16


def _cparams(sem):
    return pltpu.CompilerParams(dimension_semantics=sem, vmem_limit_bytes=V7X_VMEM_LIMIT_BYTES)


def _dot(a, b):
    return jnp.dot(a.astype(BF16), b.astype(BF16), preferred_element_type=F32)


def _dot_nt(a, b):
    return lax.dot_general(a.astype(BF16), b.astype(BF16), (((1,), (1,)), ((), ())), preferred_element_type=F32)


def _dot_tn(a, b):
    return lax.dot_general(a.astype(BF16), b.astype(BF16), (((0,), (0,)), ((), ())), preferred_element_type=F32)


def _split2(a):
    hi = a.astype(BF16)
    lo = (a - hi.astype(F32)).astype(BF16)
    return hi, lo


def _split3(a):
    p1 = a.astype(BF16)
    r1 = a - p1.astype(F32)
    p2 = r1.astype(BF16)
    p3 = (r1 - p2.astype(F32)).astype(BF16)
    return p1, p2, p3


def _dot3(a, b):
    ah, al = _split2(a)
    bh, bl = _split2(b)
    return _dot(ah, bh) + _dot(ah, bl) + _dot(al, bh)


def _dot3_nt(a, b):
    ah, al = _split2(a)
    bh, bl = _split2(b)
    return _dot_nt(ah, bh) + _dot_nt(ah, bl) + _dot_nt(al, bh)


def _sigmoid(x):
    return 1.0 / (1.0 + jnp.exp(-x))


def _rms(x, w):
    return x * lax.rsqrt(jnp.mean(x * x, axis=-1, keepdims=True) + EPS) * w


def _ada_kernel(c_ref, w_ref, b_ref, o_ref):
    c = c_ref[...]
    o_ref[...] = _dot3(c * _sigmoid(c), w_ref[...]) + b_ref[...]


def _ada_mod(c_all, w_ada, b_ada):
    rows, d = c_all.shape
    n = w_ada.shape[1]
    tn = 1024
    return pl.pallas_call(
        _ada_kernel,
        grid=(n // tn,),
        in_specs=[
            pl.BlockSpec((rows, d), lambda j: (0, 0)),
            pl.BlockSpec((d, tn), lambda j: (0, j)),
            pl.BlockSpec((1, tn), lambda j: (0, j)),
        ],
        out_specs=pl.BlockSpec((rows, tn), lambda j: (0, j)),
        out_shape=jax.ShapeDtypeStruct((rows, n), F32),
        compiler_params=_cparams(("arbitrary",)),
        name="ada_mod",
    )(c_all, w_ada, b_ada.reshape(1, n))


def _split_kernel(w_ref, hi_ref, lo_ref):
    hi, lo = _split2(w_ref[...])
    hi_ref[...] = hi
    lo_ref[...] = lo


def _split_weights(w):
    r, c = w.shape
    tr = 256
    return pl.pallas_call(
        _split_kernel,
        grid=(r // tr,),
        in_specs=[pl.BlockSpec((tr, c), lambda i: (i, 0))],
        out_specs=[pl.BlockSpec((tr, c), lambda i: (i, 0))] * 2,
        out_shape=[jax.ShapeDtypeStruct((r, c), BF16)] * 2,
        compiler_params=_cparams(("arbitrary",)),
        name="split_weights",
    )(w)


def _rope_group(y, cos, sin_signed):
    lane = lax.broadcasted_iota(jnp.int32, (1, LANES), 1)
    first_half = (lane % AT_DH) < (AT_DH // 2)
    rot = jnp.where(first_half, pltpu.roll(y, LANES - AT_DH // 2, 1), pltpu.roll(y, AT_DH // 2, 1))
    return y * cos + rot * sin_signed


def _in_proj_kernel(x_ref, sc_ref, sh_ref, g_ref, cos_ref, sin_ref, w_ref, wlo_ref, *outs, hgw, atw, kv_t):
    if kv_t:
        hg_ref, q_ref, k_ref, kt_ref, vt_ref, gate_ref = outs
    else:
        hg_ref, q_ref, k_ref, v_ref, gate_ref = outs
    x = x_ref[...]
    h = _rms(x, g_ref[...]) * (1.0 + sc_ref[0]) + sh_ref[0]
    hh, hl = _split2(h)
    seg = 512
    for j in range(hgw // seg):
        hg_ref[:, j * seg:(j + 1) * seg] = _dot(hh, w_ref[:, j * seg:(j + 1) * seg])
    tm = x.shape[0]
    hs = jnp.concatenate([hh, hl], axis=0)
    for idx, out in enumerate((q_ref, k_ref)):
        c0 = hgw + idx * atw
        a = jnp.dot(hs, w_ref[:, c0:c0 + atw], preferred_element_type=F32)
        y = a[:tm] + a[tm:] + _dot(hh, wlo_ref[:, idx * atw:(idx + 1) * atw])
        for g in range(atw // LANES):
            yg = _rope_group(y[:, g * LANES:(g + 1) * LANES], cos_ref[...], sin_ref[...])
            out[:, g * LANES:(g + 1) * LANES] = yg
            if kv_t and idx == 1:
                kt_ref[0, g * LANES:(g + 1) * LANES, :] = yg.T
    c0 = hgw + 2 * atw
    v = _dot(hh, w_ref[:, c0:c0 + atw])
    if kv_t:
        for g in range(atw // LANES):
            vt_ref[0, g * LANES:(g + 1) * LANES, :] = v[:, g * LANES:(g + 1) * LANES].T
    else:
        v_ref[...] = v
    c0 = hgw + 3 * atw
    gw = gate_ref.shape[1]
    for j in range(gw // seg):
        gate = _sigmoid(_dot(hh, w_ref[:, c0 + j * seg:c0 + (j + 1) * seg]))
        gate_ref[:, j * seg:(j + 1) * seg] = gate.astype(gate_ref.dtype)


def _in_proj(x, sc, sh, g_pre, cos_t, sin_t, w_hi, wqk_lo, tm, seq_len=None):
    n, d = x.shape
    kv_t = seq_len is not None
    hgw = 2 * HG_HEADS * HG_DK + 2 * HG_HEADS * HG_DV
    atw = AT_HEADS * AT_DH
    gw = 2 * d
    ncols = w_hi.shape[1]
    assert ncols == hgw + 3 * atw + gw
    nt = n // tm
    nbm, r, _ = sc.shape
    tps = nt // nbm
    ctiles = cos_t.shape[0] // tm
    resident = dict(pipeline_mode=pl.Buffered(1))
    rowspec = lambda w: pl.BlockSpec((tm, w), lambda i: (i, 0))
    rowshape = lambda w: jax.ShapeDtypeStruct((n, w), F32)
    if kv_t:
        tiles_per_seq = seq_len // tm
        tspec = pl.BlockSpec((1, atw, tm), lambda i: (i // tiles_per_seq, 0, i % tiles_per_seq))
        tshape = jax.ShapeDtypeStruct((n // seq_len, atw, seq_len), F32)
        kv_specs, kv_shapes = [rowspec(atw), tspec, tspec], [rowshape(atw), tshape, tshape]
    else:
        kv_specs, kv_shapes = [rowspec(atw), rowspec(atw)], [rowshape(atw), rowshape(atw)]
    return pl.pallas_call(
        functools.partial(_in_proj_kernel, hgw=hgw, atw=atw, kv_t=kv_t),
        grid=(nt,),
        in_specs=[
            pl.BlockSpec((tm, d), lambda i: (i, 0)),
            pl.BlockSpec((1, r, d), lambda i: (i // tps, 0, 0)),
            pl.BlockSpec((1, r, d), lambda i: (i // tps, 0, 0)),
            pl.BlockSpec((1, d), lambda i: (0, 0)),
            pl.BlockSpec((tm, LANES), lambda i: (i % ctiles, 0)),
            pl.BlockSpec((tm, LANES), lambda i: (i % ctiles, 0)),
            pl.BlockSpec((d, ncols), lambda i: (0, 0), **resident),
            pl.BlockSpec((d, 2 * atw), lambda i: (0, 0), **resident),
        ],
        out_specs=[rowspec(hgw), rowspec(atw)] + kv_specs + [rowspec(gw)],
        out_shape=[rowshape(hgw), rowshape(atw)] + kv_shapes + [jax.ShapeDtypeStruct((n, gw), BF16)],
        compiler_params=_cparams(("arbitrary",)),
        name="in_proj",
    )(x, sc, sh, g_pre.reshape(1, d), cos_t, sin_t, w_hi, wqk_lo)


def _cumsum_rows(g):
    c = g.shape[0]
    row = lax.broadcasted_iota(jnp.int32, (c, c), 0)
    col = lax.broadcasted_iota(jnp.int32, (c, c), 1)
    tri = jnp.where(row >= col, 1.0, 0.0).astype(BF16)
    p1, p2, p3 = _split3(g)
    return (jnp.dot(tri, p1, preferred_element_type=F32) + jnp.dot(tri, p2, preferred_element_type=F32)
            + jnp.dot(tri, p3, preferred_element_type=F32))


def _hgrn_chunk(q, hf, v, hg, lb, gon, st, *, c, bd, valid):
    f = lb + (1.0 - lb) * _sigmoid(hf)
    gl = jnp.log(f)
    kin = 1.0 - f
    row = lax.broadcasted_iota(jnp.int32, (c, 1), 0)
    if valid < c:
        live = row < valid
        gl = jnp.where(live, gl, 0.0)
        kin = jnp.where(live, kin, 0.0)
    b = _cumsum_rows(gl)
    o = _dot_nt(q * jnp.exp(b), st)
    a = None
    m = c // 2
    while m >= bd:
        span = 2 * m
        right = (row % span) >= m
        npar = c // span
        ref = b[m - 1:m]
        for p in range(1, npar):
            ref = jnp.where(row // span == p, b[p * span + m - 1:p * span + m], ref)
        eq = jnp.exp(jnp.where(right, b - ref, NEG))
        ek = jnp.exp(jnp.where(right, NEG, ref - b))
        al = _dot_nt(q * eq, kin * ek)
        if npar > 1:
            rp = lax.broadcasted_iota(jnp.int32, (c, c), 0) // span
            cp = lax.broadcasted_iota(jnp.int32, (c, c), 1) // span
            al = jnp.where(rp == cp, al, 0.0)
        a = al if a is None else a + al
        m //= 2
    if a is not None:
        o = o + _dot(a, v)
    trow = lax.broadcasted_iota(jnp.int32, (bd, 1), 0)
    blocks = []
    for i in range(c // bd):
        sl = slice(i * bd, (i + 1) * bd)
        bi, qi, ki, vi = b[sl], q[sl], kin[sl], v[sl]
        od = jnp.zeros((bd, v.shape[1]), F32)
        for s in range(bd):
            e = jnp.exp(jnp.where(trow >= s, bi - bi[s:s + 1], NEG))
            ac = jnp.sum(qi * e * ki[s:s + 1], axis=-1, keepdims=True)
            od = od + ac * vi[s:s + 1]
        blocks.append(od)
    o = o + (blocks[0] if len(blocks) == 1 else jnp.concatenate(blocks, axis=0))
    bl = b[c - 1:c]
    st_new = st * jnp.exp(bl) + _dot_tn(v, kin * jnp.exp(bl - b))
    y = _rms(o, gon) * (hg * _sigmoid(hg))
    return y, st_new


def _hgrn_kernel(*refs, layer, c, bd, valid, n_inner, use_s0, hps):
    if use_s0:
        lb_ref, gon_ref, q_ref, f_ref, i_ref, g_ref, s0_ref, o_ref, sout_ref, st_scr = refs
    else:
        lb_ref, gon_ref, q_ref, f_ref, i_ref, g_ref, o_ref, sout_ref, st_scr = refs
    cb = pl.program_id(2)

    @pl.when(cb == 0)
    def _init():
        for hh in range(hps):
            st_scr[hh] = s0_ref[0, hh].T if use_s0 else jnp.zeros(st_scr.shape[1:], F32)

    rows = [lb_ref[j:j + 1, :] for j in range(lb_ref.shape[0])]
    mx = rows[0]
    for r in rows[1:]:
        mx = jnp.maximum(mx, r)
    es = [jnp.exp(r - mx) for r in rows]
    tot = es[0]
    for e in es[1:]:
        tot = tot + e
    part = es[0]
    for e in es[1:layer + 1]:
        part = part + e
    lb_all = part / tot
    gon = gon_ref[...]

    for hh in range(hps):
        ln = slice(hh * HG_DK, (hh + 1) * HG_DK)
        lb = lb_all[:, ln]
        st = st_scr[hh]
        for j in range(n_inner):
            sl = slice(j * c, (j + 1) * c)
            y, st = _hgrn_chunk(q_ref[sl, ln], f_ref[sl, ln], i_ref[sl, ln], g_ref[sl, ln], lb, gon, st,
                                c=c, bd=bd, valid=valid)
            o_ref[sl, ln] = y.astype(o_ref.dtype)
        st_scr[hh] = st

    @pl.when(cb == pl.num_programs(2) - 1)
    def _fin():
        for hh in range(hps):
            sout_ref[0, hh] = st_scr[hh].T


def _hgrn(hg4, hg_lb, g_onorm, s0, *, layer, bsz, t, c, bd, valid, rows_per_step, hps, out_dtype):
    n = hg4.shape[0]
    h = HG_HEADS
    hgroups = h // hps
    n_inner = rows_per_step // c
    steps = t // rows_per_step
    use_s0 = s0 is not None

    def col(k):
        return pl.BlockSpec((rows_per_step, hps * HG_DK), lambda b, hg, cb, k=k: (b * steps + cb, k * hgroups + hg))

    state_spec = pl.BlockSpec((1, hps, HG_DK, HG_DV), lambda b, hg, cb: (b, hg, 0, 0))
    in_specs = [
        pl.BlockSpec((hg_lb.shape[0], hps * HG_DK), lambda b, hg, cb: (0, hg)),
        pl.BlockSpec((1, HG_DV), lambda b, hg, cb: (0, 0)),
        col(0), col(1), col(2), col(3),
    ]
    args = [hg_lb, g_onorm.reshape(1, HG_DV), hg4, hg4, hg4, hg4]
    if use_s0:
        in_specs.append(state_spec)
        args.append(s0)
    return pl.pallas_call(
        functools.partial(_hgrn_kernel, layer=layer, c=c, bd=bd, valid=valid, n_inner=n_inner, use_s0=use_s0,
                          hps=hps),
        grid=(bsz, hgroups, steps),
        in_specs=in_specs,
        out_specs=[
            pl.BlockSpec((rows_per_step, hps * HG_DV), lambda b, hg, cb: (b * steps + cb, hg)),
            state_spec,
        ],
        out_shape=[
            jax.ShapeDtypeStruct((n, h * HG_DV), out_dtype),
            jax.ShapeDtypeStruct((bsz, h, HG_DK, HG_DV), F32),
        ],
        scratch_shapes=[pltpu.VMEM((hps, HG_DV, HG_DK), F32)],
        compiler_params=_cparams(("arbitrary", "arbitrary", "arbitrary")),
        name="hgrn2",
    )(*args)


def _topk_select_t(g, own, nblk):
    rown = lax.broadcasted_iota(jnp.int32, (g.shape[0], 1), 0)
    past = rown < own
    gm = jnp.where(past, g, NEG)
    rank = jnp.zeros(g.shape, F32)
    for m in range(nblk - 1):
        grow = gm[m:m + 1, :]
        tie = jnp.where(rown > m, 1.0, 0.0)
        rank = rank + jnp.where(grow > gm, 1.0, 0.0) + jnp.where(grow == gm, tie, 0.0)
    return jnp.where(past, jnp.where(rank < MOBA_TOPK, 1.0, 0.0), 0.0)


def _attn_kernel(q_ref, k_ref, vt_ref, o_ref, km_scr, qh_scr, sel_scr, s_scr, acc_scr, *, nblk):
    tq = MOBA_BLOCK
    own = pl.program_id(1)
    scale = AT_DH ** -0.5
    ngrp = AT_HEADS // HEADS_PER_VREG

    @pl.when(own == 0)
    def _means():
        km_scr[...] = jnp.zeros_like(km_scr)
        for n in range(nblk):
            blk = k_ref[n * MOBA_BLOCK:(n + 1) * MOBA_BLOCK, :]
            km_scr[n:n + 1, :] = jnp.sum(blk, axis=0, keepdims=True) * (1.0 / MOBA_BLOCK)

    lane = lax.broadcasted_iota(jnp.int32, (1, LANES), 1)
    rown = lax.broadcasted_iota(jnp.int32, (km_scr.shape[0], 1), 0)
    causal = (lax.broadcasted_iota(jnp.int32, (MOBA_BLOCK, tq), 0)
              <= lax.broadcasted_iota(jnp.int32, (MOBA_BLOCK, tq), 1))
    r_own = pl.multiple_of(own * MOBA_BLOCK, MOBA_BLOCK)

    ms = []
    for grp in range(ngrp):
        lo = grp * LANES
        qp = q_ref[:, lo:lo + LANES]
        kmp = km_scr[:, lo:lo + LANES]
        kb = k_ref[pl.ds(r_own, MOBA_BLOCK), lo:lo + LANES].astype(BF16)
        for sub in range(HEADS_PER_VREG):
            h = grp * HEADS_PER_VREG + sub
            qh = jnp.where((lane // AT_DH) == sub, qp, 0.0) * scale
            sel_scr[h] = _topk_select_t(_dot3_nt(kmp, qh), own, nblk)
            qhb = (qh * LOG2E).astype(BF16)
            qh_scr[h] = qhb
            st = jnp.where(causal, _dot_nt(kb, qhb), NEG)
            s_scr[h, own] = st
            ms.append(jnp.max(st, axis=0, keepdims=True))

    def pass_a(n, ms):
        rows = pl.ds(pl.multiple_of(n * MOBA_BLOCK, MOBA_BLOCK), MOBA_BLOCK)
        out = []
        for grp in range(ngrp):
            kb = k_ref[rows, grp * LANES:(grp + 1) * LANES].astype(BF16)
            for sub in range(HEADS_PER_VREG):
                h = grp * HEADS_PER_VREG + sub
                selrow = jnp.max(jnp.where(rown == n, sel_scr[h], 0.0), axis=0, keepdims=True)
                st = jnp.where(selrow > 0.0, _dot_nt(kb, qh_scr[h]), NEG)
                s_scr[h, n] = st
                out.append(jnp.maximum(ms[h], jnp.max(st, axis=0, keepdims=True)))
        return tuple(out)

    ms = lax.fori_loop(0, own, pass_a, tuple(ms))

    acc_scr[...] = jnp.zeros_like(acc_scr)
    ones = jnp.ones((acc_scr.shape[1] - AT_DH, MOBA_BLOCK), BF16)

    def pass_b(n, carry):
        cols = pl.ds(pl.multiple_of(n * MOBA_BLOCK, MOBA_BLOCK), MOBA_BLOCK)
        for h in range(AT_HEADS):
            p = jnp.exp2(s_scr[h, n] - ms[h]).astype(BF16)
            vte = jnp.concatenate([vt_ref[0, h * AT_DH:(h + 1) * AT_DH, cols].astype(BF16), ones], axis=0)
            acc_scr[h] = acc_scr[h] + jnp.dot(vte, p, preferred_element_type=F32)
        return carry

    lax.fori_loop(0, own + 1, pass_b, 0)
    for grp in range(ngrp):
        parts = []
        for h in range(grp * HEADS_PER_VREG, (grp + 1) * HEADS_PER_VREG):
            acc = acc_scr[h]
            parts.append(acc[:AT_DH] / acc[AT_DH:AT_DH + 1])
        ot = jnp.concatenate(parts, axis=0)
        o_ref[:, grp * LANES:(grp + 1) * LANES] = ot.T.astype(o_ref.dtype)


def _attn_prompt(q, k, vt, *, bsz, t):
    n, w = q.shape
    nblk = t // MOBA_BLOCK
    assert t % MOBA_BLOCK == 0
    rpad = -(-nblk // SUBLANES) * SUBLANES
    return pl.pallas_call(
        functools.partial(_attn_kernel, nblk=nblk),
        grid=(bsz, nblk),
        in_specs=[
            pl.BlockSpec((MOBA_BLOCK, w), lambda b, i: (b * nblk + i, 0)),
            pl.BlockSpec((t, w), lambda b, i: (b, 0)),
            pl.BlockSpec((1, w, t), lambda b, i: (b, 0, 0)),
        ],
        out_specs=pl.BlockSpec((MOBA_BLOCK, w), lambda b, i: (b * nblk + i, 0)),
        out_shape=jax.ShapeDtypeStruct((n, w), BF16),
        scratch_shapes=[pltpu.VMEM((rpad, w), F32),
                        pltpu.VMEM((AT_HEADS, MOBA_BLOCK, LANES), BF16),
                        pltpu.VMEM((AT_HEADS, rpad, MOBA_BLOCK), F32),
                        pltpu.VMEM((AT_HEADS, nblk, MOBA_BLOCK, MOBA_BLOCK), F32),
                        pltpu.VMEM((AT_HEADS, AT_DH + 2 * SUBLANES, MOBA_BLOCK), F32)],
        compiler_params=_cparams(("arbitrary", "arbitrary")),
        name="moba_prompt",
    )(q, k, vt)


def _samp_scores_kernel(*refs, npg, pages_per_blk):
    pt_ref, qbd_ref, qm_ref, kn_ref, vn_ref = refs[:5]
    kpages = refs[5:5 + npg]
    p_ref, oown_ref, pk_ref, s_scr = refs[5 + npg:]
    del pt_ref
    g = pl.program_id(1)
    nh, dh, ps = kpages[0].shape[2:]
    rows = qm_ref.shape[1]
    scale = dh ** -0.5

    qh, ql = _split2(qbd_ref[0] * scale)
    qs = jnp.concatenate([qh, ql], axis=0)
    grp = 2 if npg % 2 == 0 else 1
    for j in range(0, npg, grp):
        kp = jnp.concatenate([kpages[j + i][0, 0].reshape(nh * dh, ps) for i in range(grp)], axis=1)
        kh, kl = _split2(kp)
        a = jnp.dot(qs, kh, preferred_element_type=F32)
        s = a[:rows] + a[rows:] + jnp.dot(qh, kl, preferred_element_type=F32)
        s_scr[:, pl.ds(pl.multiple_of((g * npg + j) * ps, ps), grp * ps)] = s

    @pl.when(g == pl.num_programs(1) - 1)
    def _finish():
        total = s_scr.shape[1]
        bw = pages_per_blk * ps
        nblk = total // bw
        lane = lax.broadcasted_iota(jnp.int32, (1, LANES), 1)
        gate = jnp.full((rows, LANES), NEG, F32)
        for n in range(nblk):
            col = jnp.sum(s_scr[:, n * bw:(n + 1) * bw], axis=-1, keepdims=True) * (1.0 / bw)
            gate = jnp.where(lane == n, col, gate)
        lanef = lane.astype(F32)
        cur = gate
        picks = []
        for _ in range(min(MOBA_TOPK, nblk)):
            mx = jnp.max(cur, axis=-1, keepdims=True)
            idx = jnp.min(jnp.where(cur == mx, lanef, float(LANES)), axis=-1, keepdims=True)
            picks.append(idx)
            cur = jnp.where(lanef == idx, NEG, cur)
        pk = jnp.zeros((rows, LANES), F32)
        for j, idx in enumerate(picks):
            pk = jnp.where(lane == j, idx, pk)
        pk_ref[0] = pk.astype(jnp.int32)
        qm = qm_ref[0] * scale
        so = _dot_nt(qm, kn_ref[0])
        ro = lax.broadcasted_iota(jnp.int32, (rows, 1), 0)
        co = lax.broadcasted_iota(jnp.int32, (1, rows), 1)
        own_ok = jnp.logical_and(ro % nh == co % nh, co // nh <= ro // nh)
        so = jnp.where(own_ok, so, NEG)
        m = jnp.max(so, axis=-1, keepdims=True)
        lchunk = 8 * bw if total % (8 * bw) == 0 else bw
        nch = total // lchunk
        for cidx in range(nch):
            cs = slice(cidx * lchunk, (cidx + 1) * lchunk)
            bid = (cidx * (lchunk // bw) + lax.broadcasted_iota(jnp.int32, (1, lchunk), 1) // bw).astype(F32)
            hit = jnp.zeros((rows, lchunk), F32)
            for idx in picks:
                hit = jnp.where(bid == idx, 1.0, hit)
            sc = jnp.where(hit > 0.0, s_scr[:, cs], NEG)
            s_scr[:, cs] = sc
            m = jnp.maximum(m, jnp.max(sc, axis=-1, keepdims=True))
        po = jnp.exp(so - m)
        l = jnp.sum(po, axis=-1, keepdims=True)
        for cidx in range(nch):
            cs = slice(cidx * lchunk, (cidx + 1) * lchunk)
            pc = jnp.exp(s_scr[:, cs] - m)
            s_scr[:, cs] = pc
            l = l + jnp.sum(pc, axis=-1, keepdims=True)
        inv = 1.0 / l
        for cidx in range(nch):
            cs = slice(cidx * lchunk, (cidx + 1) * lchunk)
            p_ref[0, :, cs] = s_scr[:, cs] * inv
        oown_ref[0] = _dot(po * inv, vn_ref[0])


def _samp_pv_kernel(pt_ref, pk_ref, p_ref, oown_ref, cv_hbm, o_ref, vbuf, sem, *, layer, topk, pages_per_blk):
    b = pl.program_id(0)
    rows = p_ref.shape[1]
    nh, dh, ps = cv_hbm.shape[2:]

    def slice_copy(r, j, pg):
        page = pt_ref[b, pk_ref[b, r * topk + j] * pages_per_blk + pg]
        return pltpu.make_async_copy(cv_hbm.at[layer, page, r % nh], vbuf.at[r, j * pages_per_blk + pg], sem.at[r])

    slots = [(j, pg) for j in range(topk) for pg in range(pages_per_blk)]
    for r in range(rows):
        for j, pg in slots:
            slice_copy(r, j, pg).start()

    lane = lax.broadcasted_iota(jnp.int32, (1, LANES), 1)
    ot = jnp.zeros((dh, LANES), F32)
    for r in range(rows):
        for j, pg in slots:
            slice_copy(r, j, pg).wait()
        acc = jnp.zeros((dh, ps), F32)
        for j, pg in slots:
            off = pl.multiple_of((pk_ref[b, r * topk + j] * pages_per_blk + pg) * ps, ps)
            acc = acc + vbuf[r, j * pages_per_blk + pg] * p_ref[0, r:r + 1, pl.ds(off, ps)]
        ot = jnp.where(lane == r, jnp.sum(acc, axis=-1, keepdims=True), ot)
    o_ref[0] = oown_ref[0] + ot.T[:rows, :]


def _attn_sample(qm, kn, vn, cache_k, cache_v, page_table, *, layer):
    bs, rows, dh = qm.shape
    _, _, ps, nh, _ = cache_k.shape
    npages = page_table.shape[1]
    pages_per_blk = MOBA_BLOCK // ps
    assert MOBA_BLOCK % ps == 0 and npages % pages_per_blk == 0 and npages // pages_per_blk <= LANES
    npg = math.gcd(npages, PAGES_PER_STEP)
    total = npages * ps
    ck = jnp.transpose(cache_k, (0, 1, 3, 4, 2))
    cv = jnp.transpose(cache_v, (0, 1, 3, 4, 2))
    head_of_row = jnp.arange(rows, dtype=jnp.int32) % nh
    head_of_col = jnp.arange(nh * dh, dtype=jnp.int32) // dh
    qbd = jnp.where(head_of_row[:, None] == head_of_col[None, :], jnp.tile(qm, (1, 1, nh)), 0.0)

    def page_spec(j):
        return pl.BlockSpec((1, 1, nh, dh, ps), lambda b, g, pt, j=j: (layer, pt[b, g * npg + j], 0, 0, 0))

    row_spec = pl.BlockSpec((1, rows, dh), lambda b, g, pt: (b, 0, 0))
    topk = min(MOBA_TOPK, npages // pages_per_blk)
    assert rows <= LANES
    probs, o_own, picks = pl.pallas_call(
        functools.partial(_samp_scores_kernel, npg=npg, pages_per_blk=pages_per_blk),
        grid_spec=pltpu.PrefetchScalarGridSpec(
            num_scalar_prefetch=1,
            grid=(bs, npages // npg),
            in_specs=[pl.BlockSpec((1, rows, nh * dh), lambda b, g, pt: (b, 0, 0)), row_spec, row_spec, row_spec]
            + [page_spec(j) for j in range(npg)],
            out_specs=[pl.BlockSpec((1, rows, total), lambda b, g, pt: (b, 0, 0)), row_spec,
                       pl.BlockSpec((1, rows, LANES), lambda b, g, pt: (b, 0, 0))],
            scratch_shapes=[pltpu.VMEM((rows, total), F32)],
        ),
        out_shape=[jax.ShapeDtypeStruct((bs, rows, total), F32), jax.ShapeDtypeStruct((bs, rows, dh), F32),
                   jax.ShapeDtypeStruct((bs, rows, LANES), jnp.int32)],
        compiler_params=_cparams(("arbitrary", "arbitrary")),
        name="moba_sample_scores",
    )(page_table, qbd, qm, kn, vn, *([ck] * npg))
    picks = picks[:, :, :topk].reshape(bs, rows * topk)
    seq_spec = lambda w: pl.BlockSpec((1, rows, w), lambda b, pt, pk: (b, 0, 0))
    return pl.pallas_call(
        functools.partial(_samp_pv_kernel, layer=layer, topk=topk, pages_per_blk=pages_per_blk),
        grid_spec=pltpu.PrefetchScalarGridSpec(
            num_scalar_prefetch=2,
            grid=(bs,),
            in_specs=[seq_spec(total), seq_spec(dh), pl.BlockSpec(memory_space=pl.ANY)],
            out_specs=seq_spec(dh),
            scratch_shapes=[pltpu.VMEM((rows, topk * pages_per_blk, dh, ps), F32), pltpu.SemaphoreType.DMA((rows,))],
        ),
        out_shape=jax.ShapeDtypeStruct((bs, rows, dh), F32),
        compiler_params=_cparams(("arbitrary",)),
        name="moba_sample_pv",
    )(page_table, picks, probs, o_own, cv)


def _mix_ffn_kernel(x_ref, oa_ref, ob_ref, gate_ref, gt1_ref, sc2_ref, sh2_ref, gt2_ref,
                    gpost1_ref, gpre2_ref, gpost2_ref, wa_ref, wb_ref, wo_ref, wgu_ref, wd_ref, out_ref,
                    *, dff, fchunk):
    d = x_ref.shape[1]
    merged = gate_ref[:, :d] * _dot(oa_ref[...], wa_ref[...]) + gate_ref[:, d:] * _dot(ob_ref[...], wb_ref[...])
    x1 = x_ref[...] + gt1_ref[0] * _rms(_dot(merged, wo_ref[...]), gpost1_ref[...])
    hb = (_rms(x1, gpre2_ref[...]) * (1.0 + sc2_ref[0]) + sh2_ref[0]).astype(BF16)
    acc = jnp.zeros(x1.shape, F32)
    for j in range(dff // fchunk):
        g = _dot(hb, wgu_ref[:, j * fchunk:(j + 1) * fchunk])
        u = _dot(hb, wgu_ref[:, dff + j * fchunk:dff + (j + 1) * fchunk])
        acc = acc + _dot(g * _sigmoid(g) * u, wd_ref[j * fchunk:(j + 1) * fchunk, :])
    out_ref[...] = x1 + gt2_ref[0] * _rms(acc, gpost2_ref[...])


def _mix_ffn(x, oa, ob, gates, gt1, sc2, sh2, gt2, g_post1, g_pre2, g_post2, wa, wb, wo, wgu, wd, tm):
    n, d = x.shape
    dff = wd.shape[0]
    nt = n // tm
    nbm, r, _ = gt1.shape
    tps = nt // nbm
    fchunk = dff // 2 if (dff // 2) % LANES == 0 else dff
    resident = dict(pipeline_mode=pl.Buffered(1))
    rowspec = lambda w: pl.BlockSpec((tm, w), lambda i: (i, 0))
    modspec = pl.BlockSpec((1, r, d), lambda i: (i // tps, 0, 0))
    vecspec = pl.BlockSpec((1, d), lambda i: (0, 0))
    wspec = lambda w: pl.BlockSpec(w.shape, lambda i: (0, 0), **resident)
    return pl.pallas_call(
        functools.partial(_mix_ffn_kernel, dff=dff, fchunk=fchunk),
        grid=(nt,),
        in_specs=[rowspec(d), rowspec(oa.shape[1]), rowspec(ob.shape[1]), rowspec(gates.shape[1]),
                  modspec, modspec, modspec, modspec, vecspec, vecspec, vecspec,
                  wspec(wa), wspec(wb), wspec(wo), wspec(wgu), wspec(wd)],
        out_specs=rowspec(d),
        out_shape=jax.ShapeDtypeStruct((n, d), F32),
        compiler_params=_cparams(("arbitrary",)),
        name="merge_proj_ffn",
    )(x, oa, ob, gates, gt1, sc2, sh2, gt2, g_post1.reshape(1, d), g_pre2.reshape(1, d), g_post2.reshape(1, d),
      wa, wb, wo, wgu, wd)


def _rope_tables(pos):
    half = AT_DH // 2
    inv = ROPE_THETA ** (-jnp.arange(half, dtype=F32) / half)
    ang = pos.astype(F32)[:, None] * inv[None, :]
    cos = jnp.cos(ang)
    sin = jnp.sin(ang)
    cos_h = jnp.concatenate([cos, cos], axis=-1)
    sin_h = jnp.concatenate([-sin, sin], axis=-1)
    return jnp.tile(cos_h, (1, HEADS_PER_VREG)), jnp.tile(sin_h, (1, HEADS_PER_VREG))


def _group_layer(x2d, mods, *, bsz, t, pos0, s0, paged, layer, weights, tm, hg_cfg):
    sh1, sc1, gt1, sh2, sc2, gt2 = mods
    (hg_lb, g_pre1, g_post1, w_in_hi, wqk_lo, g_onorm, wa, wb, wo, g_pre2, g_post2, wgu, wd) = weights
    n = x2d.shape[0]
    cos_t, sin_t = _rope_tables(pos0 + jnp.arange(t, dtype=jnp.int32))
    if tm > t:
        cos_t = jnp.tile(cos_t, (tm // t, 1))
        sin_t = jnp.tile(sin_t, (tm // t, 1))
    if paged is None:
        hg4, q, k, kt, vt, gates = _in_proj(x2d, sc1, sh1, g_pre1, cos_t, sin_t, w_in_hi, wqk_lo, tm, seq_len=t)
    else:
        hg4, q, k, v, gates = _in_proj(x2d, sc1, sh1, g_pre1, cos_t, sin_t, w_in_hi, wqk_lo, tm)

    c, bd, rows_per_step, hps = hg_cfg
    tp = -(-t // c) * c
    if tp != t:
        hg4p = jnp.pad(hg4.reshape(bsz, t, -1), ((0, 0), (0, tp - t), (0, 0))).reshape(bsz * tp, -1)
    else:
        hg4p = hg4
    o_a, s_new = _hgrn(hg4p, hg_lb, g_onorm, s0, layer=layer, bsz=bsz, t=tp, c=c, bd=bd,
                       valid=c - (tp - t), rows_per_step=rows_per_step, hps=hps,
                       out_dtype=BF16 if rows_per_step % (2 * SUBLANES) == 0 else F32)
    if tp != t:
        o_a = o_a.reshape(bsz, tp, -1)[:, :t].reshape(n, -1)

    if paged is None:
        o_b = _attn_prompt(q, k, vt, bsz=bsz, t=t)
        from_t = lambda a: jnp.transpose(a.reshape(bsz, AT_HEADS, AT_DH, t), (0, 3, 1, 2))
        k_out, v_out = from_t(kt), from_t(vt)
    else:
        cache_k, cache_v, page_table = paged
        to_rows = lambda a: a.reshape(bsz, t * AT_HEADS, AT_DH)
        o_b = _attn_sample(to_rows(q), to_rows(k), to_rows(v), cache_k, cache_v, page_table, layer=layer)
        o_b = o_b.reshape(n, AT_HEADS * AT_DH)
        k_out, v_out = k.reshape(bsz, t, AT_HEADS, AT_DH), v.reshape(bsz, t, AT_HEADS, AT_DH)

    x2 = _mix_ffn(x2d, o_a, o_b, gates, gt1, sc2, sh2, gt2, g_post1, g_pre2, g_post2, wa, wb, wo, wgu, wd, tm)
    return x2, s_new, k_out, v_out


def kernel(x_prompt, x_sample, c_prompt, c_sample, state_hgrn, cache_k, cache_v, page_table, hg_lb, w_ada, b_ada,
           g_pre1, g_post1, w_in, g_onorm, w_proj_a, w_proj_b, w_out, g_pre2, g_post2, w_gu, w_down):
    bp, t, d = x_prompt.shape
    bs, ts, _ = x_sample.shape
    depth = w_in.shape[0]
    past_len = page_table.shape[1] * cache_k.shape[2]
    assert past_len % MOBA_BLOCK == 0 and ts <= MOBA_BLOCK
    hgw = 2 * HG_HEADS * HG_DK + 2 * HG_HEADS * HG_DV
    atw = AT_HEADS * AT_DH

    xp = x_prompt.reshape(bp * t, d)
    xs = x_sample.reshape(bs * ts, d)
    c_all = jnp.concatenate([c_prompt, c_sample], axis=0)
    outs = [[] for _ in range(6)]
    for l in range(depth):
        mod = _ada_mod(c_all, w_ada[l], b_ada[l])
        mods_p = [m[:bp].reshape(bp, 1, d) for m in jnp.split(mod, 6, axis=-1)]
        mods_s = [jnp.repeat(m[bp:], ts, axis=0).reshape(1, bs * ts, d) for m in jnp.split(mod, 6, axis=-1)]
        _, wqk_lo = _split_weights(w_in[l][:, hgw:hgw + 2 * atw])
        weights = (hg_lb, g_pre1[l], g_post1[l], w_in[l].astype(BF16), wqk_lo, g_onorm[l],
                   w_proj_a[l].astype(BF16), w_proj_b[l].astype(BF16), w_out[l].astype(BF16),
                   g_pre2[l], g_post2[l], w_gu[l].astype(BF16), w_down[l].astype(BF16))
        chunk = HG_CHUNK if t % HG_CHUNK == 0 else t
        xp, sp, kp, vp = _group_layer(xp, mods_p, bsz=bp, t=t, pos0=0, s0=None, paged=None, layer=l,
                                      weights=weights, tm=512, hg_cfg=(chunk, 8, 512, 1))
        cs = -(-ts // SUBLANES) * SUBLANES
        xs, ss, ksn, vsn = _group_layer(xs, mods_s, bsz=bs, t=ts, pos0=past_len, s0=state_hgrn[l],
                                        paged=(cache_k, cache_v, page_table), layer=l,
                                        weights=weights, tm=bs * ts, hg_cfg=(cs, cs, cs, HG_HEADS))
        for lst, val in zip(outs, (sp, ss, kp, vp, ksn, vsn)):
            lst.append(val)
    return (xp.reshape(bp, t, d), xs.reshape(bs, ts, d)) + tuple(jnp.stack(o) for o in outs)
```

```python
import functools
import math

import jax
import jax.numpy as jnp
from jax import lax
from jax.experimental import pallas as pl
from jax.experimental.pallas import tpu as pltpu

F32 = jnp.float32
BF16 = jnp.bfloat16

HG_HEADS = 4
HG_DK = 128
HG_DV = 128
HG_CHUNK = 128
AT_HEADS = 8
AT_DH = 64
MOBA_BLOCK = 256
MOBA_TOPK = 3
ROPE_THETA = 10000.0
EPS = 1e-6
NEG = -1e30
LOG2E = math.log2(math.e)

V7X_VMEM_LIMIT_BYTES = 52 * 1024 * 1024
LANES = 128
SUBLANES = 8
HEADS_PER_VREG = LANES // AT_DH
MXU_WIDTH = 256
PAGES_PER_STEP = 16


def _cparams(sem):
    return pltpu.CompilerParams(dimension_semantics=sem, vmem_limit_bytes=V7X_VMEM_LIMIT_BYTES)


def _dot(a, b):
    return jnp.dot(a.astype(BF16), b.astype(BF16), preferred_element_type=F32)


def _dot_nt(a, b):
    return lax.dot_general(a.astype(BF16), b.astype(BF16), (((1,), (1,)), ((), ())), preferred_element_type=F32)


def _dot_tn(a, b):
    return lax.dot_general(a.astype(BF16), b.astype(BF16), (((0,), (0,)), ((), ())), preferred_element_type=F32)


def _split2(a):
    hi = a.astype(BF16)
    lo = (a - hi.astype(F32)).astype(BF16)
    return hi, lo


def _split3(a):
    p1 = a.astype(BF16)
    r1 = a - p1.astype(F32)
    p2 = r1.astype(BF16)
    p3 = (r1 - p2.astype(F32)).astype(BF16)
    return p1, p2, p3


def _dot3(a, b):
    ah, al = _split2(a)
    bh, bl = _split2(b)
    return _dot(ah, bh) + _dot(ah, bl) + _dot(al, bh)


def _dot3_nt(a, b):
    ah, al = _split2(a)
    bh, bl = _split2(b)
    return _dot_nt(ah, bh) + _dot_nt(ah, bl) + _dot_nt(al, bh)


def _sigmoid(x):
    return 1.0 / (1.0 + jnp.exp(-x))


def _rms(x, w):
    return x * lax.rsqrt(jnp.mean(x * x, axis=-1, keepdims=True) + EPS) * w


def _ada_kernel(c_ref, w_ref, b_ref, o_ref):
    c = c_ref[...]
    o_ref[...] = _dot3(c * _sigmoid(c), w_ref[...]) + b_ref[...]


def _ada_mod(c_all, w_ada, b_ada):
    rows, d = c_all.shape
    n = w_ada.shape[1]
    tn = 1024
    return pl.pallas_call(
        _ada_kernel,
        grid=(n // tn,),
        in_specs=[
            pl.BlockSpec((rows, d), lambda j: (0, 0)),
            pl.BlockSpec((d, tn), lambda j: (0, j)),
            pl.BlockSpec((1, tn), lambda j: (0, j)),
        ],
        out_specs=pl.BlockSpec((rows, tn), lambda j: (0, j)),
        out_shape=jax.ShapeDtypeStruct((rows, n), F32),
        compiler_params=_cparams(("arbitrary",)),
        name="ada_mod",
    )(c_all, w_ada, b_ada.reshape(1, n))


def _split_kernel(w_ref, hi_ref, lo_ref):
    hi, lo = _split2(w_ref[...])
    hi_ref[...] = hi
    lo_ref[...] = lo


def _split_weights(w):
    r, c = w.shape
    tr = 256
    return pl.pallas_call(
        _split_kernel,
        grid=(r // tr,),
        in_specs=[pl.BlockSpec((tr, c), lambda i: (i, 0))],
        out_specs=[pl.BlockSpec((tr, c), lambda i: (i, 0))] * 2,
        out_shape=[jax.ShapeDtypeStruct((r, c), BF16)] * 2,
        compiler_params=_cparams(("arbitrary",)),
        name="split_weights",
    )(w)


def _rope_group(y, cos, sin_signed):
    lane = lax.broadcasted_iota(jnp.int32, (1, LANES), 1)
    first_half = (lane % AT_DH) < (AT_DH // 2)
    rot = jnp.where(first_half, pltpu.roll(y, LANES - AT_DH // 2, 1), pltpu.roll(y, AT_DH // 2, 1))
    return y * cos + rot * sin_signed


def _in_proj_kernel(x_ref, sc_ref, sh_ref, g_ref, cos_ref, sin_ref, w_ref, wlo_ref, *outs, hgw, atw, kv_t):
    if kv_t:
        hg_ref, q_ref, k_ref, kt_ref, vt_ref, gate_ref = outs
    else:
        hg_ref, q_ref, k_ref, v_ref, gate_ref = outs
    x = x_ref[...]
    h = _rms(x, g_ref[...]) * (1.0 + sc_ref[0]) + sh_ref[0]
    hh, hl = _split2(h)
    seg = 512
    for j in range(hgw // seg):
        hg_ref[:, j * seg:(j + 1) * seg] = _dot(hh, w_ref[:, j * seg:(j + 1) * seg])
    tm = x.shape[0]
    hs = jnp.concatenate([hh, hl], axis=0)
    for idx, out in enumerate((q_ref, k_ref)):
        c0 = hgw + idx * atw
        a = jnp.dot(hs, w_ref[:, c0:c0 + atw], preferred_element_type=F32)
        y = a[:tm] + a[tm:] + _dot(hh, wlo_ref[:, idx * atw:(idx + 1) * atw])
        for g in range(atw // LANES):
            yg = _rope_group(y[:, g * LANES:(g + 1) * LANES], cos_ref[...], sin_ref[...])
            out[:, g * LANES:(g + 1) * LANES] = yg
            if kv_t and idx == 1:
                kt_ref[0, g * LANES:(g + 1) * LANES, :] = yg.T
    c0 = hgw + 2 * atw
    v = _dot(hh, w_ref[:, c0:c0 + atw])
    if kv_t:
        for g in range(atw // LANES):
            vt_ref[0, g * LANES:(g + 1) * LANES, :] = v[:, g * LANES:(g + 1) * LANES].T
    else:
        v_ref[...] = v
    c0 = hgw + 3 * atw
    gw = gate_ref.shape[1]
    for j in range(gw // seg):
        gate = _sigmoid(_dot(hh, w_ref[:, c0 + j * seg:c0 + (j + 1) * seg]))
        gate_ref[:, j * seg:(j + 1) * seg] = gate.astype(gate_ref.dtype)


def _in_proj(x, sc, sh, g_pre, cos_t, sin_t, w_hi, wqk_lo, tm, seq_len=None):
    n, d = x.shape
    kv_t = seq_len is not None
    hgw = 2 * HG_HEADS * HG_DK + 2 * HG_HEADS * HG_DV
    atw = AT_HEADS * AT_DH
    gw = 2 * d
    ncols = w_hi.shape[1]
    assert ncols == hgw + 3 * atw + gw
    nt = n // tm
    nbm, r, _ = sc.shape
    tps = nt // nbm
    ctiles = cos_t.shape[0] // tm
    resident = dict(pipeline_mode=pl.Buffered(1))
    rowspec = lambda w: pl.BlockSpec((tm, w), lambda i: (i, 0))
    rowshape = lambda w: jax.ShapeDtypeStruct((n, w), F32)
    if kv_t:
        tiles_per_seq = seq_len // tm
        tspec = pl.BlockSpec((1, atw, tm), lambda i: (i // tiles_per_seq, 0, i % tiles_per_seq))
        tshape = jax.ShapeDtypeStruct((n // seq_len, atw, seq_len), F32)
        kv_specs, kv_shapes = [rowspec(atw), tspec, tspec], [rowshape(atw), tshape, tshape]
    else:
        kv_specs, kv_shapes = [rowspec(atw), rowspec(atw)], [rowshape(atw), rowshape(atw)]
    return pl.pallas_call(
        functools.partial(_in_proj_kernel, hgw=hgw, atw=atw, kv_t=kv_t),
        grid=(nt,),
        in_specs=[
            pl.BlockSpec((tm, d), lambda i: (i, 0)),
            pl.BlockSpec((1, r, d), lambda i: (i // tps, 0, 0)),
            pl.BlockSpec((1, r, d), lambda i: (i // tps, 0, 0)),
            pl.BlockSpec((1, d), lambda i: (0, 0)),
            pl.BlockSpec((tm, LANES), lambda i: (i % ctiles, 0)),
            pl.BlockSpec((tm, LANES), lambda i: (i % ctiles, 0)),
            pl.BlockSpec((d, ncols), lambda i: (0, 0), **resident),
            pl.BlockSpec((d, 2 * atw), lambda i: (0, 0), **resident),
        ],
        out_specs=[rowspec(hgw), rowspec(atw)] + kv_specs + [rowspec(gw)],
        out_shape=[rowshape(hgw), rowshape(atw)] + kv_shapes + [jax.ShapeDtypeStruct((n, gw), BF16)],
        compiler_params=_cparams(("arbitrary",)),
        name="in_proj",
    )(x, sc, sh, g_pre.reshape(1, d), cos_t, sin_t, w_hi, wqk_lo)


def _cumsum_rows(g):
    c = g.shape[0]
    row = lax.broadcasted_iota(jnp.int32, (c, c), 0)
    col = lax.broadcasted_iota(jnp.int32, (c, c), 1)
    tri = jnp.where(row >= col, 1.0, 0.0).astype(BF16)
    p1, p2, p3 = _split3(g)
    return (jnp.dot(tri, p1, preferred_element_type=F32) + jnp.dot(tri, p2, preferred_element_type=F32)
            + jnp.dot(tri, p3, preferred_element_type=F32))


def _hgrn_chunk(q, hf, v, hg, lb, gon, st, *, c, bd, valid):
    f = lb + (1.0 - lb) * _sigmoid(hf)
    gl = jnp.log(f)
    kin = 1.0 - f
    row = lax.broadcasted_iota(jnp.int32, (c, 1), 0)
    if valid < c:
        live = row < valid
        gl = jnp.where(live, gl, 0.0)
        kin = jnp.where(live, kin, 0.0)
    b = _cumsum_rows(gl)
    o = _dot_nt(q * jnp.exp(b), st)
    a = None
    m = c // 2
    while m >= bd:
        span = 2 * m
        right = (row % span) >= m
        npar = c // span
        ref = b[m - 1:m]
        for p in range(1, npar):
            ref = jnp.where(row // span == p, b[p * span + m - 1:p * span + m], ref)
        eq = jnp.exp(jnp.where(right, b - ref, NEG))
        ek = jnp.exp(jnp.where(right, NEG, ref - b))
        al = _dot_nt(q * eq, kin * ek)
        if npar > 1:
            rp = lax.broadcasted_iota(jnp.int32, (c, c), 0) // span
            cp = lax.broadcasted_iota(jnp.int32, (c, c), 1) // span
            al = jnp.where(rp == cp, al, 0.0)
        a = al if a is None else a + al
        m //= 2
    if a is not None:
        o = o + _dot(a, v)
    trow = lax.broadcasted_iota(jnp.int32, (bd, 1), 0)
    blocks = []
    for i in range(c // bd):
        sl = slice(i * bd, (i + 1) * bd)
        bi, qi, ki, vi = b[sl], q[sl], kin[sl], v[sl]
        od = jnp.zeros((bd, v.shape[1]), F32)
        for s in range(bd):
            e = jnp.exp(jnp.where(trow >= s, bi - bi[s:s + 1], NEG))
            ac = jnp.sum(qi * e * ki[s:s + 1], axis=-1, keepdims=True)
            od = od + ac * vi[s:s + 1]
        blocks.append(od)
    o = o + (blocks[0] if len(blocks) == 1 else jnp.concatenate(blocks, axis=0))
    bl = b[c - 1:c]
    st_new = st * jnp.exp(bl) + _dot_tn(v, kin * jnp.exp(bl - b))
    y = _rms(o, gon) * (hg * _sigmoid(hg))
    return y, st_new


def _hgrn_kernel(*refs, layer, c, bd, valid, n_inner, use_s0, hps):
    if use_s0:
        lb_ref, gon_ref, q_ref, f_ref, i_ref, g_ref, s0_ref, o_ref, sout_ref, st_scr = refs
    else:
        lb_ref, gon_ref, q_ref, f_ref, i_ref, g_ref, o_ref, sout_ref, st_scr = refs
    cb = pl.program_id(2)

    @pl.when(cb == 0)
    def _init():
        for hh in range(hps):
            st_scr[hh] = s0_ref[0, hh].T if use_s0 else jnp.zeros(st_scr.shape[1:], F32)

    rows = [lb_ref[j:j + 1, :] for j in range(lb_ref.shape[0])]
    mx = rows[0]
    for r in rows[1:]:
        mx = jnp.maximum(mx, r)
    es = [jnp.exp(r - mx) for r in rows]
    tot = es[0]
    for e in es[1:]:
        tot = tot + e
    part = es[0]
    for e in es[1:layer + 1]:
        part = part + e
    lb_all = part / tot
    gon = gon_ref[...]

    for hh in range(hps):
        ln = slice(hh * HG_DK, (hh + 1) * HG_DK)
        lb = lb_all[:, ln]
        st = st_scr[hh]
        for j in range(n_inner):
            sl = slice(j * c, (j + 1) * c)
            y, st = _hgrn_chunk(q_ref[sl, ln], f_ref[sl, ln], i_ref[sl, ln], g_ref[sl, ln], lb, gon, st,
                                c=c, bd=bd, valid=valid)
            o_ref[sl, ln] = y.astype(o_ref.dtype)
        st_scr[hh] = st

    @pl.when(cb == pl.num_programs(2) - 1)
    def _fin():
        for hh in range(hps):
            sout_ref[0, hh] = st_scr[hh].T


def _hgrn(hg4, hg_lb, g_onorm, s0, *, layer, bsz, t, c, bd, valid, rows_per_step, hps, out_dtype):
    n = hg4.shape[0]
    h = HG_HEADS
    hgroups = h // hps
    n_inner = rows_per_step // c
    steps = t // rows_per_step
    use_s0 = s0 is not None

    def col(k):
        return pl.BlockSpec((rows_per_step, hps * HG_DK), lambda b, hg, cb, k=k: (b * steps + cb, k * hgroups + hg))

    state_spec = pl.BlockSpec((1, hps, HG_DK, HG_DV), lambda b, hg, cb: (b, hg, 0, 0))
    in_specs = [
        pl.BlockSpec((hg_lb.shape[0], hps * HG_DK), lambda b, hg, cb: (0, hg)),
        pl.BlockSpec((1, HG_DV), lambda b, hg, cb: (0, 0)),
        col(0), col(1), col(2), col(3),
    ]
    args = [hg_lb, g_onorm.reshape(1, HG_DV), hg4, hg4, hg4, hg4]
    if use_s0:
        in_specs.append(state_spec)
        args.append(s0)
    return pl.pallas_call(
        functools.partial(_hgrn_kernel, layer=layer, c=c, bd=bd, valid=valid, n_inner=n_inner, use_s0=use_s0,
                          hps=hps),
        grid=(bsz, hgroups, steps),
        in_specs=in_specs,
        out_specs=[
            pl.BlockSpec((rows_per_step, hps * HG_DV), lambda b, hg, cb: (b * steps + cb, hg)),
            state_spec,
        ],
        out_shape=[
            jax.ShapeDtypeStruct((n, h * HG_DV), out_dtype),
            jax.ShapeDtypeStruct((bsz, h, HG_DK, HG_DV), F32),
        ],
        scratch_shapes=[pltpu.VMEM((hps, HG_DV, HG_DK), F32)],
        compiler_params=_cparams(("arbitrary", "arbitrary", "arbitrary")),
        name="hgrn2",
    )(*args)


def _topk_select_t(g, own, nblk):
    rown = lax.broadcasted_iota(jnp.int32, (g.shape[0], 1), 0)
    past = rown < own
    gm = jnp.where(past, g, NEG)
    rank = jnp.zeros(g.shape, F32)
    for m in range(nblk - 1):
        grow = gm[m:m + 1, :]
        tie = jnp.where(rown > m, 1.0, 0.0)
        rank = rank + jnp.where(grow > gm, 1.0, 0.0) + jnp.where(grow == gm, tie, 0.0)
    return jnp.where(past, jnp.where(rank < MOBA_TOPK, 1.0, 0.0), 0.0)


def _attn_kernel(q_ref, k_ref, vt_ref, o_ref, km_scr, qh_scr, sel_scr, s_scr, acc_scr, *, nblk):
    tq = MOBA_BLOCK
    own = pl.program_id(1)
    scale = AT_DH ** -0.5
    ngrp = AT_HEADS // HEADS_PER_VREG

    @pl.when(own == 0)
    def _means():
        km_scr[...] = jnp.zeros_like(km_scr)
        for n in range(nblk):
            blk = k_ref[n * MOBA_BLOCK:(n + 1) * MOBA_BLOCK, :]
            km_scr[n:n + 1, :] = jnp.sum(blk, axis=0, keepdims=True) * (1.0 / MOBA_BLOCK)

    lane = lax.broadcasted_iota(jnp.int32, (1, LANES), 1)
    rown = lax.broadcasted_iota(jnp.int32, (km_scr.shape[0], 1), 0)
    causal = (lax.broadcasted_iota(jnp.int32, (MOBA_BLOCK, tq), 0)
              <= lax.broadcasted_iota(jnp.int32, (MOBA_BLOCK, tq), 1))
    r_own = pl.multiple_of(own * MOBA_BLOCK, MOBA_BLOCK)

    ms = []
    for grp in range(ngrp):
        lo = grp * LANES
        qp = q_ref[:, lo:lo + LANES]
        kmp = km_scr[:, lo:lo + LANES]
        kb = k_ref[pl.ds(r_own, MOBA_BLOCK), lo:lo + LANES].astype(BF16)
        for sub in range(HEADS_PER_VREG):
            h = grp * HEADS_PER_VREG + sub
            qh = jnp.where((lane // AT_DH) == sub, qp, 0.0) * scale
            sel_scr[h] = _topk_select_t(_dot3_nt(kmp, qh), own, nblk)
            qhb = (qh * LOG2E).astype(BF16)
            qh_scr[h] = qhb
            st = jnp.where(causal, _dot_nt(kb, qhb), NEG)
            s_scr[h, own] = st
            ms.append(jnp.max(st, axis=0, keepdims=True))

    def pass_a(n, ms):
        rows = pl.ds(pl.multiple_of(n * MOBA_BLOCK, MOBA_BLOCK), MOBA_BLOCK)
        out = []
        for grp in range(ngrp):
            kb = k_ref[rows, grp * LANES:(grp + 1) * LANES].astype(BF16)
            for sub in range(HEADS_PER_VREG):
                h = grp * HEADS_PER_VREG + sub
                selrow = jnp.max(jnp.where(rown == n, sel_scr[h], 0.0), axis=0, keepdims=True)
                st = jnp.where(selrow > 0.0, _dot_nt(kb, qh_scr[h]), NEG)
                s_scr[h, n] = st
                out.append(jnp.maximum(ms[h], jnp.max(st, axis=0, keepdims=True)))
        return tuple(out)

    ms = lax.fori_loop(0, own, pass_a, tuple(ms))

    acc_scr[...] = jnp.zeros_like(acc_scr)
    ones = jnp.ones((acc_scr.shape[1] - AT_DH, MOBA_BLOCK), BF16)

    def pass_b(n, carry):
        cols = pl.ds(pl.multiple_of(n * MOBA_BLOCK, MOBA_BLOCK), MOBA_BLOCK)
        for h in range(AT_HEADS):
            p = jnp.exp2(s_scr[h, n] - ms[h]).astype(BF16)
            vte = jnp.concatenate([vt_ref[0, h * AT_DH:(h + 1) * AT_DH, cols].astype(BF16), ones], axis=0)
            acc_scr[h] = acc_scr[h] + jnp.dot(vte, p, preferred_element_type=F32)
        return carry

    lax.fori_loop(0, own + 1, pass_b, 0)
    for grp in range(ngrp):
        parts = []
        for h in range(grp * HEADS_PER_VREG, (grp + 1) * HEADS_PER_VREG):
            acc = acc_scr[h]
            parts.append(acc[:AT_DH] / acc[AT_DH:AT_DH + 1])
        ot = jnp.concatenate(parts, axis=0)
        o_ref[:, grp * LANES:(grp + 1) * LANES] = ot.T.astype(o_ref.dtype)


def _attn_prompt(q, k, vt, *, bsz, t):
    n, w = q.shape
    nblk = t // MOBA_BLOCK
    assert t % MOBA_BLOCK == 0
    rpad = -(-nblk // SUBLANES) * SUBLANES
    return pl.pallas_call(
        functools.partial(_attn_kernel, nblk=nblk),
        grid=(bsz, nblk),
        in_specs=[
            pl.BlockSpec((MOBA_BLOCK, w), lambda b, i: (b * nblk + i, 0)),
            pl.BlockSpec((t, w), lambda b, i: (b, 0)),
            pl.BlockSpec((1, w, t), lambda b, i: (b, 0, 0)),
        ],
        out_specs=pl.BlockSpec((MOBA_BLOCK, w), lambda b, i: (b * nblk + i, 0)),
        out_shape=jax.ShapeDtypeStruct((n, w), BF16),
        scratch_shapes=[pltpu.VMEM((rpad, w), F32),
                        pltpu.VMEM((AT_HEADS, MOBA_BLOCK, LANES), BF16),
                        pltpu.VMEM((AT_HEADS, rpad, MOBA_BLOCK), F32),
                        pltpu.VMEM((AT_HEADS, nblk, MOBA_BLOCK, MOBA_BLOCK), F32),
                        pltpu.VMEM((AT_HEADS, AT_DH + 2 * SUBLANES, MOBA_BLOCK), F32)],
        compiler_params=_cparams(("arbitrary", "arbitrary")),
        name="moba_prompt",
    )(q, k, vt)


def _samp_scores_kernel(*refs, npg, pages_per_blk):
    pt_ref, qbd_ref, qm_ref, kn_ref, vn_ref = refs[:5]
    kpages = refs[5:5 + npg]
    p_ref, oown_ref, pk_ref, s_scr = refs[5 + npg:]
    del pt_ref
    g = pl.program_id(1)
    nh, dh, ps = kpages[0].shape[2:]
    rows = qm_ref.shape[1]
    scale = dh ** -0.5

    qh, ql = _split2(qbd_ref[0] * scale)
    qs = jnp.concatenate([qh, ql], axis=0)
    grp = 2 if npg % 2 == 0 else 1
    for j in range(0, npg, grp):
        kp = jnp.concatenate([kpages[j + i][0, 0].reshape(nh * dh, ps) for i in range(grp)], axis=1)
        kh, kl = _split2(kp)
        a = jnp.dot(qs, kh, preferred_element_type=F32)
        s = a[:rows] + a[rows:] + jnp.dot(qh, kl, preferred_element_type=F32)
        s_scr[:, pl.ds(pl.multiple_of((g * npg + j) * ps, ps), grp * ps)] = s

    @pl.when(g == pl.num_programs(1) - 1)
    def _finish():
        total = s_scr.shape[1]
        bw = pages_per_blk * ps
        nblk = total // bw
        lane = lax.broadcasted_iota(jnp.int32, (1, LANES), 1)
        gate = jnp.full((rows, LANES), NEG, F32)
        for n in range(nblk):
            col = jnp.sum(s_scr[:, n * bw:(n + 1) * bw], axis=-1, keepdims=True) * (1.0 / bw)
            gate = jnp.where(lane == n, col, gate)
        lanef = lane.astype(F32)
        cur = gate
        picks = []
        for _ in range(min(MOBA_TOPK, nblk)):
            mx = jnp.max(cur, axis=-1, keepdims=True)
            idx = jnp.min(jnp.where(cur == mx, lanef, float(LANES)), axis=-1, keepdims=True)
            picks.append(idx)
            cur = jnp.where(lanef == idx, NEG, cur)
        pk = jnp.zeros((rows, LANES), F32)
        for j, idx in enumerate(picks):
            pk = jnp.where(lane == j, idx, pk)
        pk_ref[0] = pk.astype(jnp.int32)
        qm = qm_ref[0] * scale
        so = _dot_nt(qm, kn_ref[0])
        ro = lax.broadcasted_iota(jnp.int32, (rows, 1), 0)
        co = lax.broadcasted_iota(jnp.int32, (1, rows), 1)
        own_ok = jnp.logical_and(ro % nh == co % nh, co // nh <= ro // nh)
        so = jnp.where(own_ok, so, NEG)
        m = jnp.max(so, axis=-1, keepdims=True)
        lchunk = 8 * bw if total % (8 * bw) == 0 else bw
        nch = total // lchunk
        for cidx in range(nch):
            cs = slice(cidx * lchunk, (cidx + 1) * lchunk)
            bid = (cidx * (lchunk // bw) + lax.broadcasted_iota(jnp.int32, (1, lchunk), 1) // bw).astype(F32)
            hit = jnp.zeros((rows, lchunk), F32)
            for idx in picks:
                hit = jnp.where(bid == idx, 1.0, hit)
            sc = jnp.where(hit > 0.0, s_scr[:, cs], NEG)
            s_scr[:, cs] = sc
            m = jnp.maximum(m, jnp.max(sc, axis=-1, keepdims=True))
        po = jnp.exp(so - m)
        l = jnp.sum(po, axis=-1, keepdims=True)
        for cidx in range(nch):
            cs = slice(cidx * lchunk, (cidx + 1) * lchunk)
            pc = jnp.exp(s_scr[:, cs] - m)
            s_scr[:, cs] = pc
            l = l + jnp.sum(pc, axis=-1, keepdims=True)
        inv = 1.0 / l
        for cidx in range(nch):
            cs = slice(cidx * lchunk, (cidx + 1) * lchunk)
            p_ref[0, :, cs] = s_scr[:, cs] * inv
        oown_ref[0] = _dot(po * inv, vn_ref[0])


def _samp_pv_kernel(pt_ref, pk_ref, p_ref, oown_ref, cv_hbm, o_ref, vbuf, sem, *, layer, topk, pages_per_blk):
    b = pl.program_id(0)
    rows = p_ref.shape[1]
    nh, dh, ps = cv_hbm.shape[2:]

    def slice_copy(r, j, pg):
        page = pt_ref[b, pk_ref[b, r * topk + j] * pages_per_blk + pg]
        return pltpu.make_async_copy(cv_hbm.at[layer, page, r % nh], vbuf.at[r, j * pages_per_blk + pg], sem.at[r])

    slots = [(j, pg) for j in range(topk) for pg in range(pages_per_blk)]
    for r in range(rows):
        for j, pg in slots:
            slice_copy(r, j, pg).start()

    lane = lax.broadcasted_iota(jnp.int32, (1, LANES), 1)
    ot = jnp.zeros((dh, LANES), F32)
    for r in range(rows):
        for j, pg in slots:
            slice_copy(r, j, pg).wait()
        acc = jnp.zeros((dh, ps), F32)
        for j, pg in slots:
            off = pl.multiple_of((pk_ref[b, r * topk + j] * pages_per_blk + pg) * ps, ps)
            acc = acc + vbuf[r, j * pages_per_blk + pg] * p_ref[0, r:r + 1, pl.ds(off, ps)]
        ot = jnp.where(lane == r, jnp.sum(acc, axis=-1, keepdims=True), ot)
    o_ref[0] = oown_ref[0] + ot.T[:rows, :]


def _attn_sample(qm, kn, vn, cache_k, cache_v, page_table, *, layer):
    bs, rows, dh = qm.shape
    _, _, ps, nh, _ = cache_k.shape
    npages = page_table.shape[1]
    pages_per_blk = MOBA_BLOCK // ps
    assert MOBA_BLOCK % ps == 0 and npages % pages_per_blk == 0 and npages // pages_per_blk <= LANES
    npg = math.gcd(npages, PAGES_PER_STEP)
    total = npages * ps
    ck = jnp.transpose(cache_k, (0, 1, 3, 4, 2))
    cv = jnp.transpose(cache_v, (0, 1, 3, 4, 2))
    head_of_row = jnp.arange(rows, dtype=jnp.int32) % nh
    head_of_col = jnp.arange(nh * dh, dtype=jnp.int32) // dh
    qbd = jnp.where(head_of_row[:, None] == head_of_col[None, :], jnp.tile(qm, (1, 1, nh)), 0.0)

    def page_spec(j):
        return pl.BlockSpec((1, 1, nh, dh, ps), lambda b, g, pt, j=j: (layer, pt[b, g * npg + j], 0, 0, 0))

    row_spec = pl.BlockSpec((1, rows, dh), lambda b, g, pt: (b, 0, 0))
    topk = min(MOBA_TOPK, npages // pages_per_blk)
    assert rows <= LANES
    probs, o_own, picks = pl.pallas_call(
        functools.partial(_samp_scores_kernel, npg=npg, pages_per_blk=pages_per_blk),
        grid_spec=pltpu.PrefetchScalarGridSpec(
            num_scalar_prefetch=1,
            grid=(bs, npages // npg),
            in_specs=[pl.BlockSpec((1, rows, nh * dh), lambda b, g, pt: (b, 0, 0)), row_spec, row_spec, row_spec]
            + [page_spec(j) for j in range(npg)],
            out_specs=[pl.BlockSpec((1, rows, total), lambda b, g, pt: (b, 0, 0)), row_spec,
                       pl.BlockSpec((1, rows, LANES), lambda b, g, pt: (b, 0, 0))],
            scratch_shapes=[pltpu.VMEM((rows, total), F32)],
        ),
        out_shape=[jax.ShapeDtypeStruct((bs, rows, total), F32), jax.ShapeDtypeStruct((bs, rows, dh), F32),
                   jax.ShapeDtypeStruct((bs, rows, LANES), jnp.int32)],
        compiler_params=_cparams(("arbitrary", "arbitrary")),
        name="moba_sample_scores",
    )(page_table, qbd, qm, kn, vn, *([ck] * npg))
    picks = picks[:, :, :topk].reshape(bs, rows * topk)
    seq_spec = lambda w: pl.BlockSpec((1, rows, w), lambda b, pt, pk: (b, 0, 0))
    return pl.pallas_call(
        functools.partial(_samp_pv_kernel, layer=layer, topk=topk, pages_per_blk=pages_per_blk),
        grid_spec=pltpu.PrefetchScalarGridSpec(
            num_scalar_prefetch=2,
            grid=(bs,),
            in_specs=[seq_spec(total), seq_spec(dh), pl.BlockSpec(memory_space=pl.ANY)],
            out_specs=seq_spec(dh),
            scratch_shapes=[pltpu.VMEM((rows, topk * pages_per_blk, dh, ps), F32), pltpu.SemaphoreType.DMA((rows,))],
        ),
        out_shape=jax.ShapeDtypeStruct((bs, rows, dh), F32),
        compiler_params=_cparams(("arbitrary",)),
        name="moba_sample_pv",
    )(page_table, picks, probs, o_own, cv)


def _mix_ffn_kernel(x_ref, oa_ref, ob_ref, gate_ref, gt1_ref, sc2_ref, sh2_ref, gt2_ref,
                    gpost1_ref, gpre2_ref, gpost2_ref, wa_ref, wb_ref, wo_ref, wgu_ref, wd_ref, out_ref,
                    *, dff, fchunk):
    d = x_ref.shape[1]
    merged = gate_ref[:, :d] * _dot(oa_ref[...], wa_ref[...]) + gate_ref[:, d:] * _dot(ob_ref[...], wb_ref[...])
    x1 = x_ref[...] + gt1_ref[0] * _rms(_dot(merged, wo_ref[...]), gpost1_ref[...])
    hb = (_rms(x1, gpre2_ref[...]) * (1.0 + sc2_ref[0]) + sh2_ref[0]).astype(BF16)
    acc = jnp.zeros(x1.shape, F32)
    for c0 in range(0, dff, fchunk):
        c1 = min(c0 + fchunk, dff)
        g = _dot(hb, wgu_ref[:, c0:c1])
        u = _dot(hb, wgu_ref[:, dff + c0:dff + c1])
        acc = acc + _dot(g * _sigmoid(g) * u, wd_ref[c0:c1, :])
    out_ref[...] = x1 + gt2_ref[0] * _rms(acc, gpost2_ref[...])


def _mix_ffn(x, oa, ob, gates, gt1, sc2, sh2, gt2, g_post1, g_pre2, g_post2, wa, wb, wo, wgu, wd, tm):
    n, d = x.shape
    dff = wd.shape[0]
    nt = n // tm
    nbm, r, _ = gt1.shape
    tps = nt // nbm
    fchunk = 3 * MXU_WIDTH
    resident = dict(pipeline_mode=pl.Buffered(1))
    rowspec = lambda w: pl.BlockSpec((tm, w), lambda i: (i, 0))
    modspec = pl.BlockSpec((1, r, d), lambda i: (i // tps, 0, 0))
    vecspec = pl.BlockSpec((1, d), lambda i: (0, 0))
    wspec = lambda w: pl.BlockSpec(w.shape, lambda i: (0, 0), **resident)
    return pl.pallas_call(
        functools.partial(_mix_ffn_kernel, dff=dff, fchunk=fchunk),
        grid=(nt,),
        in_specs=[rowspec(d), rowspec(oa.shape[1]), rowspec(ob.shape[1]), rowspec(gates.shape[1]),
                  modspec, modspec, modspec, modspec, vecspec, vecspec, vecspec,
                  wspec(wa), wspec(wb), wspec(wo), wspec(wgu), wspec(wd)],
        out_specs=rowspec(d),
        out_shape=jax.ShapeDtypeStruct((n, d), F32),
        compiler_params=_cparams(("arbitrary",)),
        name="merge_proj_ffn",
    )(x, oa, ob, gates, gt1, sc2, sh2, gt2, g_post1.reshape(1, d), g_pre2.reshape(1, d), g_post2.reshape(1, d),
      wa, wb, wo, wgu, wd)


def _rope_tables(pos):
    half = AT_DH // 2
    inv = ROPE_THETA ** (-jnp.arange(half, dtype=F32) / half)
    ang = pos.astype(F32)[:, None] * inv[None, :]
    cos = jnp.cos(ang)
    sin = jnp.sin(ang)
    cos_h = jnp.concatenate([cos, cos], axis=-1)
    sin_h = jnp.concatenate([-sin, sin], axis=-1)
    return jnp.tile(cos_h, (1, HEADS_PER_VREG)), jnp.tile(sin_h, (1, HEADS_PER_VREG))


def _group_layer(x2d, mods, *, bsz, t, pos0, s0, paged, layer, weights, tm, hg_cfg):
    sh1, sc1, gt1, sh2, sc2, gt2 = mods
    (hg_lb, g_pre1, g_post1, w_in_hi, wqk_lo, g_onorm, wa, wb, wo, g_pre2, g_post2, wgu, wd) = weights
    n = x2d.shape[0]
    cos_t, sin_t = _rope_tables(pos0 + jnp.arange(t, dtype=jnp.int32))
    if tm > t:
        cos_t = jnp.tile(cos_t, (tm // t, 1))
        sin_t = jnp.tile(sin_t, (tm // t, 1))
    if paged is None:
        hg4, q, k, kt, vt, gates = _in_proj(x2d, sc1, sh1, g_pre1, cos_t, sin_t, w_in_hi, wqk_lo, tm, seq_len=t)
    else:
        hg4, q, k, v, gates = _in_proj(x2d, sc1, sh1, g_pre1, cos_t, sin_t, w_in_hi, wqk_lo, tm)

    c, bd, rows_per_step, hps = hg_cfg
    tp = -(-t // c) * c
    if tp != t:
        hg4p = jnp.pad(hg4.reshape(bsz, t, -1), ((0, 0), (0, tp - t), (0, 0))).reshape(bsz * tp, -1)
    else:
        hg4p = hg4
    o_a, s_new = _hgrn(hg4p, hg_lb, g_onorm, s0, layer=layer, bsz=bsz, t=tp, c=c, bd=bd,
                       valid=c - (tp - t), rows_per_step=rows_per_step, hps=hps,
                       out_dtype=BF16 if rows_per_step % (2 * SUBLANES) == 0 else F32)
    if tp != t:
        o_a = o_a.reshape(bsz, tp, -1)[:, :t].reshape(n, -1)

    if paged is None:
        o_b = _attn_prompt(q, k, vt, bsz=bsz, t=t)
        from_t = lambda a: jnp.transpose(a.reshape(bsz, AT_HEADS, AT_DH, t), (0, 3, 1, 2))
        k_out, v_out = from_t(kt), from_t(vt)
    else:
        cache_k, cache_v, page_table = paged
        to_rows = lambda a: a.reshape(bsz, t * AT_HEADS, AT_DH)
        o_b = _attn_sample(to_rows(q), to_rows(k), to_rows(v), cache_k, cache_v, page_table, layer=layer)
        o_b = o_b.reshape(n, AT_HEADS * AT_DH)
        k_out, v_out = k.reshape(bsz, t, AT_HEADS, AT_DH), v.reshape(bsz, t, AT_HEADS, AT_DH)

    x2 = _mix_ffn(x2d, o_a, o_b, gates, gt1, sc2, sh2, gt2, g_post1, g_pre2, g_post2, wa, wb, wo, wgu, wd, tm)
    return x2, s_new, k_out, v_out


def kernel(x_prompt, x_sample, c_prompt, c_sample, state_hgrn, cache_k, cache_v, page_table, hg_lb, w_ada, b_ada,
           g_pre1, g_post1, w_in, g_onorm, w_proj_a, w_proj_b, w_out, g_pre2, g_post2, w_gu, w_down):
    bp, t, d = x_prompt.shape
    bs, ts, _ = x_sample.shape
    depth = w_in.shape[0]
    past_len = page_table.shape[1] * cache_k.shape[2]
    assert past_len % MOBA_BLOCK == 0 and ts <= MOBA_BLOCK
    hgw = 2 * HG_HEADS * HG_DK + 2 * HG_HEADS * HG_DV
    atw = AT_HEADS * AT_DH

    xp = x_prompt.reshape(bp * t, d)
    xs = x_sample.reshape(bs * ts, d)
    c_all = jnp.concatenate([c_prompt, c_sample], axis=0)
    outs = [[] for _ in range(6)]
    for l in range(depth):
        mod = _ada_mod(c_all, w_ada[l], b_ada[l])
        mods_p = [m[:bp].reshape(bp, 1, d) for m in jnp.split(mod, 6, axis=-1)]
        mods_s = [jnp.repeat(m[bp:], ts, axis=0).reshape(1, bs * ts, d) for m in jnp.split(mod, 6, axis=-1)]
        _, wqk_lo = _split_weights(w_in[l][:, hgw:hgw + 2 * atw])
        weights = (hg_lb, g_pre1[l], g_post1[l], w_in[l].astype(BF16), wqk_lo, g_onorm[l],
                   w_proj_a[l].astype(BF16), w_proj_b[l].astype(BF16), w_out[l].astype(BF16),
                   g_pre2[l], g_post2[l], w_gu[l].astype(BF16), w_down[l].astype(BF16))
        chunk = HG_CHUNK if t % HG_CHUNK == 0 else t
        xp, sp, kp, vp = _group_layer(xp, mods_p, bsz=bp, t=t, pos0=0, s0=None, paged=None, layer=l,
                                      weights=weights, tm=512, hg_cfg=(chunk, 8, 512, 1))
        cs = -(-ts // SUBLANES) * SUBLANES
        xs, ss, ksn, vsn = _group_layer(xs, mods_s, bsz=bs, t=ts, pos0=past_len, s0=state_hgrn[l],
                                        paged=(cache_k, cache_v, page_table), layer=l,
                                        weights=weights, tm=bs * ts, hg_cfg=(cs, cs, cs, HG_HEADS))
        for lst, val in zip(outs, (sp, ss, kp, vp, ksn, vsn)):
            lst.append(val)
    return (xp.reshape(bp, t, d), xs.reshape(bs, ts, d)) + tuple(jnp.stack(o) for o in outs)
```

```python
import functools
import math

import jax
import jax.numpy as jnp
from jax import lax
from jax.experimental import pallas as pl
from jax.experimental.pallas import tpu as pltpu

F32 = jnp.float32
BF16 = jnp.bfloat16

HG_HEADS = 4
HG_DK = 128
HG_DV = 128
HG_CHUNK = 128
AT_HEADS = 8
AT_DH = 64
MOBA_BLOCK = 256
MOBA_TOPK = 3
ROPE_THETA = 10000.0
EPS = 1e-6
NEG = -1e30
LOG2E = math.log2(math.e)

V7X_VMEM_LIMIT_BYTES = 52 * 1024 * 1024
LANES = 128
SUBLANES = 8
HEADS_PER_VREG = LANES // AT_DH
MXU_WIDTH = 256
PAGES_PER_STEP = 16


def _cparams(sem):
    return pltpu.CompilerParams(dimension_semantics=sem, vmem_limit_bytes=V7X_VMEM_LIMIT_BYTES)


def _dot(a, b):
    return jnp.dot(a.astype(BF16), b.astype(BF16), preferred_element_type=F32)


def _dot_nt(a, b):
    return lax.dot_general(a.astype(BF16), b.astype(BF16), (((1,), (1,)), ((), ())), preferred_element_type=F32)


def _dot_tn(a, b):
    return lax.dot_general(a.astype(BF16), b.astype(BF16), (((0,), (0,)), ((), ())), preferred_element_type=F32)


def _split2(a):
    hi = a.astype(BF16)
    lo = (a - hi.astype(F32)).astype(BF16)
    return hi, lo


def _split3(a):
    p1 = a.astype(BF16)
    r1 = a - p1.astype(F32)
    p2 = r1.astype(BF16)
    p3 = (r1 - p2.astype(F32)).astype(BF16)
    return p1, p2, p3


def _dot3(a, b):
    ah, al = _split2(a)
    bh, bl = _split2(b)
    return _dot(ah, bh) + _dot(ah, bl) + _dot(al, bh)


def _dot3_nt(a, b):
    ah, al = _split2(a)
    bh, bl = _split2(b)
    return _dot_nt(ah, bh) + _dot_nt(ah, bl) + _dot_nt(al, bh)


def _sigmoid(x):
    return 1.0 / (1.0 + jnp.exp(-x))


def _rms(x, w):
    return x * lax.rsqrt(jnp.mean(x * x, axis=-1, keepdims=True) + EPS) * w


def _ada_kernel(c_ref, w_ref, b_ref, o_ref):
    c = c_ref[...]
    o_ref[...] = _dot3(c * _sigmoid(c), w_ref[...]) + b_ref[...]


def _ada_mod(c_all, w_ada, b_ada):
    rows, d = c_all.shape
    n = w_ada.shape[1]
    tn = 1024
    return pl.pallas_call(
        _ada_kernel,
        grid=(n // tn,),
        in_specs=[
            pl.BlockSpec((rows, d), lambda j: (0, 0)),
            pl.BlockSpec((d, tn), lambda j: (0, j)),
            pl.BlockSpec((1, tn), lambda j: (0, j)),
        ],
        out_specs=pl.BlockSpec((rows, tn), lambda j: (0, j)),
        out_shape=jax.ShapeDtypeStruct((rows, n), F32),
        compiler_params=_cparams(("arbitrary",)),
        name="ada_mod",
    )(c_all, w_ada, b_ada.reshape(1, n))


def _split_kernel(w_ref, hi_ref, lo_ref):
    hi, lo = _split2(w_ref[...])
    hi_ref[...] = hi
    lo_ref[...] = lo


def _split_weights(w):
    r, c = w.shape
    tr = 256
    return pl.pallas_call(
        _split_kernel,
        grid=(r // tr,),
        in_specs=[pl.BlockSpec((tr, c), lambda i: (i, 0))],
        out_specs=[pl.BlockSpec((tr, c), lambda i: (i, 0))] * 2,
        out_shape=[jax.ShapeDtypeStruct((r, c), BF16)] * 2,
        compiler_params=_cparams(("arbitrary",)),
        name="split_weights",
    )(w)


def _rope_group(y, cos, sin_signed):
    lane = lax.broadcasted_iota(jnp.int32, (1, LANES), 1)
    first_half = (lane % AT_DH) < (AT_DH // 2)
    rot = jnp.where(first_half, pltpu.roll(y, LANES - AT_DH // 2, 1), pltpu.roll(y, AT_DH // 2, 1))
    return y * cos + rot * sin_signed


def _in_proj_kernel(x_ref, sc_ref, sh_ref, g_ref, cos_ref, sin_ref, w_ref, wlo_ref, *outs, hgw, atw, kv_t):
    if kv_t:
        hg_ref, q_ref, k_ref, kt_ref, vt_ref, gate_ref = outs
    else:
        hg_ref, q_ref, k_ref, v_ref, gate_ref = outs
    x = x_ref[...]
    h = _rms(x, g_ref[...]) * (1.0 + sc_ref[0]) + sh_ref[0]
    hh, hl = _split2(h)
    seg = 512
    for j in range(hgw // seg):
        hg_ref[:, j * seg:(j + 1) * seg] = _dot(hh, w_ref[:, j * seg:(j + 1) * seg])
    tm = x.shape[0]
    hs = jnp.concatenate([hh, hl], axis=0)
    for idx, out in enumerate((q_ref, k_ref)):
        c0 = hgw + idx * atw
        a = jnp.dot(hs, w_ref[:, c0:c0 + atw], preferred_element_type=F32)
        y = a[:tm] + a[tm:] + _dot(hh, wlo_ref[:, idx * atw:(idx + 1) * atw])
        for g in range(atw // LANES):
            yg = _rope_group(y[:, g * LANES:(g + 1) * LANES], cos_ref[...], sin_ref[...])
            out[:, g * LANES:(g + 1) * LANES] = yg
            if kv_t and idx == 1:
                kt_ref[0, g * LANES:(g + 1) * LANES, :] = yg.T
    c0 = hgw + 2 * atw
    v = _dot(hh, w_ref[:, c0:c0 + atw])
    if kv_t:
        for g in range(atw // LANES):
            vt_ref[0, g * LANES:(g + 1) * LANES, :] = v[:, g * LANES:(g + 1) * LANES].T
    else:
        v_ref[...] = v
    c0 = hgw + 3 * atw
    gw = gate_ref.shape[1]
    for j in range(gw // seg):
        gate = _sigmoid(_dot(hh, w_ref[:, c0 + j * seg:c0 + (j + 1) * seg]))
        gate_ref[:, j * seg:(j + 1) * seg] = gate.astype(gate_ref.dtype)


def _in_proj(x, sc, sh, g_pre, cos_t, sin_t, w_hi, wqk_lo, tm, seq_len=None):
    n, d = x.shape
    kv_t = seq_len is not None
    hgw = 2 * HG_HEADS * HG_DK + 2 * HG_HEADS * HG_DV
    atw = AT_HEADS * AT_DH
    gw = 2 * d
    ncols = w_hi.shape[1]
    assert ncols == hgw + 3 * atw + gw
    nt = n // tm
    nbm, r, _ = sc.shape
    tps = nt // nbm
    ctiles = cos_t.shape[0] // tm
    resident = dict(pipeline_mode=pl.Buffered(1))
    rowspec = lambda w: pl.BlockSpec((tm, w), lambda i: (i, 0))
    rowshape = lambda w: jax.ShapeDtypeStruct((n, w), F32)
    if kv_t:
        tiles_per_seq = seq_len // tm
        tspec = pl.BlockSpec((1, atw, tm), lambda i: (i // tiles_per_seq, 0, i % tiles_per_seq))
        tshape = jax.ShapeDtypeStruct((n // seq_len, atw, seq_len), F32)
        kv_specs, kv_shapes = [rowspec(atw), tspec, tspec], [rowshape(atw), tshape, tshape]
    else:
        kv_specs, kv_shapes = [rowspec(atw), rowspec(atw)], [rowshape(atw), rowshape(atw)]
    return pl.pallas_call(
        functools.partial(_in_proj_kernel, hgw=hgw, atw=atw, kv_t=kv_t),
        grid=(nt,),
        in_specs=[
            pl.BlockSpec((tm, d), lambda i: (i, 0)),
            pl.BlockSpec((1, r, d), lambda i: (i // tps, 0, 0)),
            pl.BlockSpec((1, r, d), lambda i: (i // tps, 0, 0)),
            pl.BlockSpec((1, d), lambda i: (0, 0)),
            pl.BlockSpec((tm, LANES), lambda i: (i % ctiles, 0)),
            pl.BlockSpec((tm, LANES), lambda i: (i % ctiles, 0)),
            pl.BlockSpec((d, ncols), lambda i: (0, 0), **resident),
            pl.BlockSpec((d, 2 * atw), lambda i: (0, 0), **resident),
        ],
        out_specs=[rowspec(hgw), rowspec(atw)] + kv_specs + [rowspec(gw)],
        out_shape=[rowshape(hgw), rowshape(atw)] + kv_shapes + [jax.ShapeDtypeStruct((n, gw), BF16)],
        compiler_params=_cparams(("arbitrary",)),
        name="in_proj",
    )(x, sc, sh, g_pre.reshape(1, d), cos_t, sin_t, w_hi, wqk_lo)


def _cumsum_rows(g):
    c = g.shape[0]
    row = lax.broadcasted_iota(jnp.int32, (c, c), 0)
    col = lax.broadcasted_iota(jnp.int32, (c, c), 1)
    tri = jnp.where(row >= col, 1.0, 0.0).astype(BF16)
    p1, p2, p3 = _split3(g)
    return (jnp.dot(tri, p1, preferred_element_type=F32) + jnp.dot(tri, p2, preferred_element_type=F32)
            + jnp.dot(tri, p3, preferred_element_type=F32))


def _hgrn_chunk(q, hf, v, hg, lb, gon, st, *, c, bd, valid):
    f = lb + (1.0 - lb) * _sigmoid(hf)
    gl = jnp.log(f)
    kin = 1.0 - f
    row = lax.broadcasted_iota(jnp.int32, (c, 1), 0)
    if valid < c:
        live = row < valid
        gl = jnp.where(live, gl, 0.0)
        kin = jnp.where(live, kin, 0.0)
    b = _cumsum_rows(gl)
    o = _dot_nt(q * jnp.exp(b), st)
    a = None
    m = c // 2
    while m >= bd:
        span = 2 * m
        right = (row % span) >= m
        npar = c // span
        ref = b[m - 1:m]
        for p in range(1, npar):
            ref = jnp.where(row // span == p, b[p * span + m - 1:p * span + m], ref)
        eq = jnp.exp(jnp.where(right, b - ref, NEG))
        ek = jnp.exp(jnp.where(right, NEG, ref - b))
        al = _dot_nt(q * eq, kin * ek)
        if npar > 1:
            rp = lax.broadcasted_iota(jnp.int32, (c, c), 0) // span
            cp = lax.broadcasted_iota(jnp.int32, (c, c), 1) // span
            al = jnp.where(rp == cp, al, 0.0)
        a = al if a is None else a + al
        m //= 2
    if a is not None:
        o = o + _dot(a, v)
    trow = lax.broadcasted_iota(jnp.int32, (bd, 1), 0)
    blocks = []
    for i in range(c // bd):
        sl = slice(i * bd, (i + 1) * bd)
        bi, qi, ki, vi = b[sl], q[sl], kin[sl], v[sl]
        od = jnp.zeros((bd, v.shape[1]), F32)
        for s in range(bd):
            e = jnp.exp(jnp.where(trow >= s, bi - bi[s:s + 1], NEG))
            ac = jnp.sum(qi * e * ki[s:s + 1], axis=-1, keepdims=True)
            od = od + ac * vi[s:s + 1]
        blocks.append(od)
    o = o + (blocks[0] if len(blocks) == 1 else jnp.concatenate(blocks, axis=0))
    bl = b[c - 1:c]
    st_new = st * jnp.exp(bl) + _dot_tn(v, kin * jnp.exp(bl - b))
    y = _rms(o, gon) * (hg * _sigmoid(hg))
    return y, st_new


def _hgrn_kernel(*refs, **cfg):
    _hgrn_body(refs, pl.program_id(2), pl.num_programs(2), **cfg)


def _hgrn_body(refs, cb, ncb, *, layer, c, bd, valid, n_inner, use_s0, hps):
    if use_s0:
        lb_ref, gon_ref, q_ref, f_ref, i_ref, g_ref, s0_ref, o_ref, sout_ref, st_scr = refs
    else:
        lb_ref, gon_ref, q_ref, f_ref, i_ref, g_ref, o_ref, sout_ref, st_scr = refs

    @pl.when(cb == 0)
    def _init():
        for hh in range(hps):
            st_scr[hh] = s0_ref[0, hh].T if use_s0 else jnp.zeros(st_scr.shape[1:], F32)

    rows = [lb_ref[j:j + 1, :] for j in range(lb_ref.shape[0])]
    mx = rows[0]
    for r in rows[1:]:
        mx = jnp.maximum(mx, r)
    es = [jnp.exp(r - mx) for r in rows]
    tot = es[0]
    for e in es[1:]:
        tot = tot + e
    part = es[0]
    for e in es[1:layer + 1]:
        part = part + e
    lb_all = part / tot
    gon = gon_ref[...]

    for hh in range(hps):
        ln = slice(hh * HG_DK, (hh + 1) * HG_DK)
        lb = lb_all[:, ln]
        st = st_scr[hh]
        for j in range(n_inner):
            sl = slice(j * c, (j + 1) * c)
            y, st = _hgrn_chunk(q_ref[sl, ln], f_ref[sl, ln], i_ref[sl, ln], g_ref[sl, ln], lb, gon, st,
                                c=c, bd=bd, valid=valid)
            o_ref[sl, ln] = y.astype(o_ref.dtype)
        st_scr[hh] = st

    @pl.when(cb == ncb - 1)
    def _fin():
        for hh in range(hps):
            sout_ref[0, hh] = st_scr[hh].T


def _hgrn_scores_kernel(*refs, n_hg_in, npg, hgrn_cfg, pages_per_blk):
    ins = refs[1:]
    hg_in, sc_q, kpages = ins[:n_hg_in], ins[n_hg_in:n_hg_in + 4], ins[n_hg_in + 4:n_hg_in + 4 + npg]
    rest = ins[n_hg_in + 4 + npg:]
    hg_out, sc_out, st_scr, s_scr = rest[:2], rest[2:5], rest[5], rest[6]
    _hgrn_body(tuple(hg_in) + tuple(hg_out) + (st_scr,), pl.program_id(2), pl.num_programs(2), **hgrn_cfg)
    lin = (pl.program_id(0) * pl.num_programs(1) + pl.program_id(1)) * pl.num_programs(2) + pl.program_id(2)
    ng = s_scr.shape[1] // (npg * kpages[0].shape[4])
    _scores_body(sc_q, kpages, tuple(sc_out) + (s_scr,), lin % ng, ng, pages_per_blk=pages_per_blk)


def _hgrn(hg4, hg_lb, g_onorm, s0, *, layer, bsz, t, c, bd, valid, rows_per_step, hps, out_dtype):
    plan = _hgrn_plan(hg4, hg_lb, g_onorm, s0, layer=layer, bsz=bsz, t=t, c=c, bd=bd, valid=valid,
                      rows_per_step=rows_per_step, hps=hps, out_dtype=out_dtype)
    return pl.pallas_call(
        functools.partial(_hgrn_kernel, **plan["cfg"]),
        grid=plan["grid"],
        in_specs=plan["in_specs"],
        out_specs=plan["out_specs"],
        out_shape=plan["out_shape"],
        scratch_shapes=plan["scratch"],
        compiler_params=_cparams(("arbitrary", "arbitrary", "arbitrary")),
        name="hgrn2",
    )(*plan["args"])


def _hgrn_plan(hg4, hg_lb, g_onorm, s0, *, layer, bsz, t, c, bd, valid, rows_per_step, hps, out_dtype):
    n = hg4.shape[0]
    h = HG_HEADS
    hgroups = h // hps
    steps = t // rows_per_step
    use_s0 = s0 is not None

    def col(k):
        return pl.BlockSpec((rows_per_step, hps * HG_DK),
                            lambda b, hg, cb, *_, k=k: (b * steps + cb, k * hgroups + hg))

    state_spec = pl.BlockSpec((1, hps, HG_DK, HG_DV), lambda b, hg, cb, *_: (b, hg, 0, 0))
    in_specs = [
        pl.BlockSpec((hg_lb.shape[0], hps * HG_DK), lambda b, hg, cb, *_: (0, hg)),
        pl.BlockSpec((1, HG_DV), lambda b, hg, cb, *_: (0, 0)),
        col(0), col(1), col(2), col(3),
    ]
    args = [hg_lb, g_onorm.reshape(1, HG_DV), hg4, hg4, hg4, hg4]
    if use_s0:
        in_specs.append(state_spec)
        args.append(s0)
    return dict(
        cfg=dict(layer=layer, c=c, bd=bd, valid=valid, n_inner=rows_per_step // c, use_s0=use_s0, hps=hps),
        grid=(bsz, hgroups, steps), in_specs=in_specs, args=args,
        out_specs=[pl.BlockSpec((rows_per_step, hps * HG_DV), lambda b, hg, cb, *_: (b * steps + cb, hg)),
                   state_spec],
        out_shape=[jax.ShapeDtypeStruct((n, h * HG_DV), out_dtype),
                   jax.ShapeDtypeStruct((bsz, h, HG_DK, HG_DV), F32)],
        scratch=[pltpu.VMEM((hps, HG_DV, HG_DK), F32)],
    )


def _topk_select_t(g, own, nblk):
    rown = lax.broadcasted_iota(jnp.int32, (g.shape[0], 1), 0)
    past = rown < own
    gm = jnp.where(past, g, NEG)
    rank = jnp.zeros(g.shape, F32)
    for m in range(nblk - 1):
        grow = gm[m:m + 1, :]
        tie = jnp.where(rown > m, 1.0, 0.0)
        rank = rank + jnp.where(grow > gm, 1.0, 0.0) + jnp.where(grow == gm, tie, 0.0)
    return jnp.where(past, jnp.where(rank < MOBA_TOPK, 1.0, 0.0), 0.0)


def _attn_kernel(q_ref, k_ref, vt_ref, o_ref, km_scr, qh_scr, sel_scr, s_scr, acc_scr, *, nblk):
    tq = MOBA_BLOCK
    own = pl.program_id(1)
    scale = AT_DH ** -0.5
    ngrp = AT_HEADS // HEADS_PER_VREG

    @pl.when(own == 0)
    def _means():
        km_scr[...] = jnp.zeros_like(km_scr)
        for n in range(nblk):
            blk = k_ref[n * MOBA_BLOCK:(n + 1) * MOBA_BLOCK, :]
            km_scr[n:n + 1, :] = jnp.sum(blk, axis=0, keepdims=True) * (1.0 / MOBA_BLOCK)

    lane = lax.broadcasted_iota(jnp.int32, (1, LANES), 1)
    rown = lax.broadcasted_iota(jnp.int32, (km_scr.shape[0], 1), 0)
    causal = (lax.broadcasted_iota(jnp.int32, (MOBA_BLOCK, tq), 0)
              <= lax.broadcasted_iota(jnp.int32, (MOBA_BLOCK, tq), 1))
    r_own = pl.multiple_of(own * MOBA_BLOCK, MOBA_BLOCK)

    ms = []
    for grp in range(ngrp):
        lo = grp * LANES
        qp = q_ref[:, lo:lo + LANES]
        kmp = km_scr[:, lo:lo + LANES]
        kb = k_ref[pl.ds(r_own, MOBA_BLOCK), lo:lo + LANES].astype(BF16)
        for sub in range(HEADS_PER_VREG):
            h = grp * HEADS_PER_VREG + sub
            qh = jnp.where((lane // AT_DH) == sub, qp, 0.0) * scale
            sel_scr[h] = _topk_select_t(_dot3_nt(kmp, qh), own, nblk)
            qhb = (qh * LOG2E).astype(BF16)
            qh_scr[h] = qhb
            st = jnp.where(causal, _dot_nt(kb, qhb), NEG)
            s_scr[h, own] = st
            ms.append(jnp.max(st, axis=0, keepdims=True))

    def pass_a(n, ms):
        rows = pl.ds(pl.multiple_of(n * MOBA_BLOCK, MOBA_BLOCK), MOBA_BLOCK)
        out = []
        for grp in range(ngrp):
            kb = k_ref[rows, grp * LANES:(grp + 1) * LANES].astype(BF16)
            for sub in range(HEADS_PER_VREG):
                h = grp * HEADS_PER_VREG + sub
                selrow = jnp.max(jnp.where(rown == n, sel_scr[h], 0.0), axis=0, keepdims=True)
                st = jnp.where(selrow > 0.0, _dot_nt(kb, qh_scr[h]), NEG)
                s_scr[h, n] = st
                out.append(jnp.maximum(ms[h], jnp.max(st, axis=0, keepdims=True)))
        return tuple(out)

    ms = lax.fori_loop(0, own, pass_a, tuple(ms))

    acc_scr[...] = jnp.zeros_like(acc_scr)
    ones = jnp.ones((acc_scr.shape[1] - AT_DH, MOBA_BLOCK), BF16)

    def pass_b(n, carry):
        cols = pl.ds(pl.multiple_of(n * MOBA_BLOCK, MOBA_BLOCK), MOBA_BLOCK)
        for h in range(AT_HEADS):
            p = jnp.exp2(s_scr[h, n] - ms[h]).astype(BF16)
            vte = jnp.concatenate([vt_ref[0, h * AT_DH:(h + 1) * AT_DH, cols].astype(BF16), ones], axis=0)
            acc_scr[h] = acc_scr[h] + jnp.dot(vte, p, preferred_element_type=F32)
        return carry

    lax.fori_loop(0, own + 1, pass_b, 0)
    for grp in range(ngrp):
        parts = []
        for h in range(grp * HEADS_PER_VREG, (grp + 1) * HEADS_PER_VREG):
            acc = acc_scr[h]
            parts.append(acc[:AT_DH] / acc[AT_DH:AT_DH + 1])
        ot = jnp.concatenate(parts, axis=0)
        o_ref[:, grp * LANES:(grp + 1) * LANES] = ot.T.astype(o_ref.dtype)


def _attn_prompt(q, k, vt, *, bsz, t):
    n, w = q.shape
    nblk = t // MOBA_BLOCK
    assert t % MOBA_BLOCK == 0
    rpad = -(-nblk // SUBLANES) * SUBLANES
    return pl.pallas_call(
        functools.partial(_attn_kernel, nblk=nblk),
        grid=(bsz, nblk),
        in_specs=[
            pl.BlockSpec((MOBA_BLOCK, w), lambda b, i: (b * nblk + i, 0)),
            pl.BlockSpec((t, w), lambda b, i: (b, 0)),
            pl.BlockSpec((1, w, t), lambda b, i: (b, 0, 0)),
        ],
        out_specs=pl.BlockSpec((MOBA_BLOCK, w), lambda b, i: (b * nblk + i, 0)),
        out_shape=jax.ShapeDtypeStruct((n, w), BF16),
        scratch_shapes=[pltpu.VMEM((rpad, w), F32),
                        pltpu.VMEM((AT_HEADS, MOBA_BLOCK, LANES), BF16),
                        pltpu.VMEM((AT_HEADS, rpad, MOBA_BLOCK), F32),
                        pltpu.VMEM((AT_HEADS, nblk, MOBA_BLOCK, MOBA_BLOCK), F32),
                        pltpu.VMEM((AT_HEADS, AT_DH + 2 * SUBLANES, MOBA_BLOCK), F32)],
        compiler_params=_cparams(("arbitrary", "arbitrary")),
        name="moba_prompt",
    )(q, k, vt)


def _samp_scores_kernel(*refs, npg, pages_per_blk):
    _scores_body(refs[1:5], refs[5:5 + npg], refs[5 + npg:], pl.program_id(1), pl.num_programs(1),
                 pages_per_blk=pages_per_blk)


def _scores_body(q_refs, kpages, out_refs, g, ng, *, pages_per_blk):
    qbd_ref, qm_ref, kn_ref, vn_ref = q_refs
    p_ref, oown_ref, pk_ref, s_scr = out_refs
    npg = len(kpages)
    nh, dh, ps = kpages[0].shape[2:]
    rows = qm_ref.shape[1]
    scale = dh ** -0.5

    qh, ql = _split2(qbd_ref[0] * scale)
    qs = jnp.concatenate([qh, ql], axis=0)
    grp = 2 if npg % 2 == 0 else 1
    for j in range(0, npg, grp):
        kp = jnp.concatenate([kpages[j + i][0, 0].reshape(nh * dh, ps) for i in range(grp)], axis=1)
        kh, kl = _split2(kp)
        a = jnp.dot(qs, kh, preferred_element_type=F32)
        s = a[:rows] + a[rows:] + jnp.dot(qh, kl, preferred_element_type=F32)
        s_scr[:, pl.ds(pl.multiple_of((g * npg + j) * ps, ps), grp * ps)] = s

    @pl.when(g == ng - 1)
    def _finish():
        total = s_scr.shape[1]
        bw = pages_per_blk * ps
        nblk = total // bw
        lane = lax.broadcasted_iota(jnp.int32, (1, LANES), 1)
        gate = jnp.full((rows, LANES), NEG, F32)
        for n in range(nblk):
            col = jnp.sum(s_scr[:, n * bw:(n + 1) * bw], axis=-1, keepdims=True) * (1.0 / bw)
            gate = jnp.where(lane == n, col, gate)
        lanef = lane.astype(F32)
        cur = gate
        picks = []
        for _ in range(min(MOBA_TOPK, nblk)):
            mx = jnp.max(cur, axis=-1, keepdims=True)
            idx = jnp.min(jnp.where(cur == mx, lanef, float(LANES)), axis=-1, keepdims=True)
            picks.append(idx)
            cur = jnp.where(lanef == idx, NEG, cur)
        pk = jnp.zeros((rows, LANES), F32)
        for j, idx in enumerate(picks):
            pk = jnp.where(lane == j, idx, pk)
        pk_ref[0] = pk.astype(jnp.int32)
        qm = qm_ref[0] * scale
        so = _dot_nt(qm, kn_ref[0])
        ro = lax.broadcasted_iota(jnp.int32, (rows, 1), 0)
        co = lax.broadcasted_iota(jnp.int32, (1, rows), 1)
        own_ok = jnp.logical_and(ro % nh == co % nh, co // nh <= ro // nh)
        so = jnp.where(own_ok, so, NEG)
        m = jnp.max(so, axis=-1, keepdims=True)
        lchunk = 8 * bw if total % (8 * bw) == 0 else bw
        nch = total // lchunk
        for cidx in range(nch):
            cs = slice(cidx * lchunk, (cidx + 1) * lchunk)
            bid = (cidx * (lchunk // bw) + lax.broadcasted_iota(jnp.int32, (1, lchunk), 1) // bw).astype(F32)
            hit = jnp.zeros((rows, lchunk), F32)
            for idx in picks:
                hit = jnp.where(bid == idx, 1.0, hit)
            sc = jnp.where(hit > 0.0, s_scr[:, cs], NEG)
            s_scr[:, cs] = sc
            m = jnp.maximum(m, jnp.max(sc, axis=-1, keepdims=True))
        po = jnp.exp(so - m)
        l = jnp.sum(po, axis=-1, keepdims=True)
        for cidx in range(nch):
            cs = slice(cidx * lchunk, (cidx + 1) * lchunk)
            pc = jnp.exp(s_scr[:, cs] - m)
            s_scr[:, cs] = pc
            l = l + jnp.sum(pc, axis=-1, keepdims=True)
        inv = 1.0 / l
        for cidx in range(nch):
            cs = slice(cidx * lchunk, (cidx + 1) * lchunk)
            p_ref[0, :, cs] = s_scr[:, cs] * inv
        oown_ref[0] = _dot(po * inv, vn_ref[0])


def _samp_pv_kernel(pt_ref, pk_ref, p_ref, oown_ref, cv_hbm, o_ref, vbuf, sem, *, layer, topk, pages_per_blk):
    b = pl.program_id(0)
    rows = p_ref.shape[1]
    nh, dh, ps = cv_hbm.shape[2:]

    def slice_copy(r, j, pg):
        page = pt_ref[b, pk_ref[b, r * topk + j] * pages_per_blk + pg]
        return pltpu.make_async_copy(cv_hbm.at[layer, page, r % nh], vbuf.at[r, j * pages_per_blk + pg], sem.at[r])

    slots = [(j, pg) for j in range(topk) for pg in range(pages_per_blk)]
    for r in range(rows):
        for j, pg in slots:
            slice_copy(r, j, pg).start()

    lane = lax.broadcasted_iota(jnp.int32, (1, LANES), 1)
    ot = jnp.zeros((dh, LANES), F32)
    for r in range(rows):
        for j, pg in slots:
            slice_copy(r, j, pg).wait()
        acc = jnp.zeros((dh, ps), F32)
        for j, pg in slots:
            off = pl.multiple_of((pk_ref[b, r * topk + j] * pages_per_blk + pg) * ps, ps)
            acc = acc + vbuf[r, j * pages_per_blk + pg] * p_ref[0, r:r + 1, pl.ds(off, ps)]
        ot = jnp.where(lane == r, jnp.sum(acc, axis=-1, keepdims=True), ot)
    o_ref[0] = oown_ref[0] + ot.T[:rows, :]


def _scores_plan(qm, kn, vn, cache_k, page_table, *, layer, seq_g):
    bs, rows, dh = qm.shape
    _, _, ps, nh, _ = cache_k.shape
    npages = page_table.shape[1]
    pages_per_blk = MOBA_BLOCK // ps
    assert MOBA_BLOCK % ps == 0 and npages % pages_per_blk == 0 and npages // pages_per_blk <= LANES
    assert rows <= LANES
    npg = math.gcd(npages, PAGES_PER_STEP)
    total = npages * ps
    ck = jnp.transpose(cache_k, (0, 1, 3, 4, 2))
    head_of_row = jnp.arange(rows, dtype=jnp.int32) % nh
    head_of_col = jnp.arange(nh * dh, dtype=jnp.int32) // dh
    qbd = jnp.where(head_of_row[:, None] == head_of_col[None, :], jnp.tile(qm, (1, 1, nh)), 0.0)

    def per_seq(width):
        return pl.BlockSpec((1, rows, width), lambda *idx: (seq_g(*idx[:-1])[0], 0, 0))

    def page_spec(j):
        def index_map(*idx):
            seq, g = seq_g(*idx[:-1])
            return (layer, idx[-1][seq, g * npg + j], 0, 0, 0)
        return pl.BlockSpec((1, 1, nh, dh, ps), index_map)

    return dict(
        npg=npg, pages_per_blk=pages_per_blk, groups=npages // npg, bs=bs,
        in_specs=[per_seq(nh * dh), per_seq(dh), per_seq(dh), per_seq(dh)] + [page_spec(j) for j in range(npg)],
        args=[qbd, qm, kn, vn] + [ck] * npg,
        out_specs=[per_seq(total), per_seq(dh), per_seq(LANES)],
        out_shape=[jax.ShapeDtypeStruct((bs, rows, total), F32), jax.ShapeDtypeStruct((bs, rows, dh), F32),
                   jax.ShapeDtypeStruct((bs, rows, LANES), jnp.int32)],
        scratch=[pltpu.VMEM((rows, total), F32)],
    )


def _samp_scores(plan, page_table):
    return pl.pallas_call(
        functools.partial(_samp_scores_kernel, npg=plan["npg"], pages_per_blk=plan["pages_per_blk"]),
        grid_spec=pltpu.PrefetchScalarGridSpec(
            num_scalar_prefetch=1, grid=(plan["bs"], plan["groups"]),
            in_specs=plan["in_specs"], out_specs=plan["out_specs"], scratch_shapes=plan["scratch"]),
        out_shape=plan["out_shape"],
        compiler_params=_cparams(("arbitrary", "arbitrary")),
        name="moba_sample_scores",
    )(page_table, *plan["args"])


def _hgrn_with_scores(hplan, splan, page_table):
    outs = pl.pallas_call(
        functools.partial(_hgrn_scores_kernel, n_hg_in=len(hplan["in_specs"]), npg=splan["npg"],
                          hgrn_cfg=hplan["cfg"], pages_per_blk=splan["pages_per_blk"]),
        grid_spec=pltpu.PrefetchScalarGridSpec(
            num_scalar_prefetch=1, grid=hplan["grid"],
            in_specs=hplan["in_specs"] + splan["in_specs"],
            out_specs=hplan["out_specs"] + splan["out_specs"],
            scratch_shapes=hplan["scratch"] + splan["scratch"]),
        out_shape=hplan["out_shape"] + splan["out_shape"],
        compiler_params=_cparams(("arbitrary", "arbitrary", "arbitrary")),
        name="hgrn2_with_sample_scores",
    )(page_table, *hplan["args"], *splan["args"])
    return outs[:2], outs[2:]


def _samp_pv(probs, o_own, picks, cache_v, page_table, *, layer):
    bs, rows, total = probs.shape
    _, _, ps, nh, dh = cache_v.shape
    pages_per_blk = MOBA_BLOCK // ps
    topk = min(MOBA_TOPK, page_table.shape[1] // pages_per_blk)
    cv = jnp.transpose(cache_v, (0, 1, 3, 4, 2))
    picks = picks[:, :, :topk].reshape(bs, rows * topk)
    seq_spec = lambda w: pl.BlockSpec((1, rows, w), lambda b, pt, pk: (b, 0, 0))
    return pl.pallas_call(
        functools.partial(_samp_pv_kernel, layer=layer, topk=topk, pages_per_blk=pages_per_blk),
        grid_spec=pltpu.PrefetchScalarGridSpec(
            num_scalar_prefetch=2,
            grid=(bs,),
            in_specs=[seq_spec(total), seq_spec(dh), pl.BlockSpec(memory_space=pl.ANY)],
            out_specs=seq_spec(dh),
            scratch_shapes=[pltpu.VMEM((rows, topk * pages_per_blk, dh, ps), F32), pltpu.SemaphoreType.DMA((rows,))],
        ),
        out_shape=jax.ShapeDtypeStruct((bs, rows, dh), F32),
        compiler_params=_cparams(("arbitrary",)),
        name="moba_sample_pv",
    )(page_table, picks, probs, o_own, cv)


def _mix_ffn_kernel(x_ref, oa_ref, ob_ref, gate_ref, gt1_ref, sc2_ref, sh2_ref, gt2_ref,
                    gpost1_ref, gpre2_ref, gpost2_ref, wa_ref, wb_ref, wo_ref, wgu_ref, wd_ref, out_ref,
                    *, dff, fchunk):
    d = x_ref.shape[1]
    merged = gate_ref[:, :d] * _dot(oa_ref[...], wa_ref[...]) + gate_ref[:, d:] * _dot(ob_ref[...], wb_ref[...])
    x1 = x_ref[...] + gt1_ref[0] * _rms(_dot(merged, wo_ref[...]), gpost1_ref[...])
    hb = (_rms(x1, gpre2_ref[...]) * (1.0 + sc2_ref[0]) + sh2_ref[0]).astype(BF16)
    acc = jnp.zeros(x1.shape, F32)
    for c0 in range(0, dff, fchunk):
        c1 = min(c0 + fchunk, dff)
        g = _dot(hb, wgu_ref[:, c0:c1])
        u = _dot(hb, wgu_ref[:, dff + c0:dff + c1])
        acc = acc + _dot(g * _sigmoid(g) * u, wd_ref[c0:c1, :])
    out_ref[...] = x1 + gt2_ref[0] * _rms(acc, gpost2_ref[...])


def _mix_ffn(x, oa, ob, gates, gt1, sc2, sh2, gt2, g_post1, g_pre2, g_post2, wa, wb, wo, wgu, wd, tm):
    n, d = x.shape
    dff = wd.shape[0]
    nt = n // tm
    nbm, r, _ = gt1.shape
    tps = nt // nbm
    fchunk = 3 * MXU_WIDTH
    resident = dict(pipeline_mode=pl.Buffered(1))
    rowspec = lambda w: pl.BlockSpec((tm, w), lambda i: (i, 0))
    modspec = pl.BlockSpec((1, r, d), lambda i: (i // tps, 0, 0))
    vecspec = pl.BlockSpec((1, d), lambda i: (0, 0))
    wspec = lambda w: pl.BlockSpec(w.shape, lambda i: (0, 0), **resident)
    return pl.pallas_call(
        functools.partial(_mix_ffn_kernel, dff=dff, fchunk=fchunk),
        grid=(nt,),
        in_specs=[rowspec(d), rowspec(oa.shape[1]), rowspec(ob.shape[1]), rowspec(gates.shape[1]),
                  modspec, modspec, modspec, modspec, vecspec, vecspec, vecspec,
                  wspec(wa), wspec(wb), wspec(wo), wspec(wgu), wspec(wd)],
        out_specs=rowspec(d),
        out_shape=jax.ShapeDtypeStruct((n, d), F32),
        compiler_params=_cparams(("arbitrary",)),
        name="merge_proj_ffn",
    )(x, oa, ob, gates, gt1, sc2, sh2, gt2, g_post1.reshape(1, d), g_pre2.reshape(1, d), g_post2.reshape(1, d),
      wa, wb, wo, wgu, wd)


def _rope_tables(pos):
    half = AT_DH // 2
    inv = ROPE_THETA ** (-jnp.arange(half, dtype=F32) / half)
    ang = pos.astype(F32)[:, None] * inv[None, :]
    cos = jnp.cos(ang)
    sin = jnp.sin(ang)
    cos_h = jnp.concatenate([cos, cos], axis=-1)
    sin_h = jnp.concatenate([-sin, sin], axis=-1)
    return jnp.tile(cos_h, (1, HEADS_PER_VREG)), jnp.tile(sin_h, (1, HEADS_PER_VREG))


def _project(x2d, mods, *, t, pos0, weights, tm, transposed_kv):
    sh1, sc1 = mods[0], mods[1]
    g_pre1, w_in_hi, wqk_lo = weights[1], weights[3], weights[4]
    cos_t, sin_t = _rope_tables(pos0 + jnp.arange(t, dtype=jnp.int32))
    if tm > t:
        cos_t = jnp.tile(cos_t, (tm // t, 1))
        sin_t = jnp.tile(sin_t, (tm // t, 1))
    return _in_proj(x2d, sc1, sh1, g_pre1, cos_t, sin_t, w_in_hi, wqk_lo, tm, seq_len=t if transposed_kv else None)


def _layer(xp, xs, mods_p, mods_s, s0_s, cache_k, cache_v, page_table, *, layer, weights, dims):
    bp, t, bs, ts, past_len = dims
    (hg_lb, g_pre1, g_post1, w_in_hi, wqk_lo, g_onorm, wa, wb, wo, g_pre2, g_post2, wgu, wd) = weights
    tm_p, tm_s = 512, bs * ts
    hg4_p, q_p, k_p, kt_p, vt_p, gates_p = _project(xp, mods_p, t=t, pos0=0, weights=weights, tm=tm_p,
                                                     transposed_kv=True)
    hg4_s, q_s, k_s, v_s, gates_s = _project(xs, mods_s, t=ts, pos0=past_len, weights=weights, tm=tm_s,
                                             transposed_kv=False)

    chunk = HG_CHUNK if t % HG_CHUNK == 0 else t
    hplan = _hgrn_plan(hg4_p, hg_lb, g_onorm, None, layer=layer, bsz=bp, t=t, c=chunk, bd=SUBLANES, valid=chunk,
                       rows_per_step=512, hps=1, out_dtype=BF16)
    to_rows = lambda a: a.reshape(bs, ts * AT_HEADS, AT_DH)
    groups = page_table.shape[1] // math.gcd(page_table.shape[1], PAGES_PER_STEP)
    _, hgroups, steps = hplan["grid"]
    fused = bp * hgroups * steps == bs * groups
    seq_g = ((lambda b, hg, cb: divmod((b * hgroups + hg) * steps + cb, groups)) if fused
             else (lambda b, g: (b, g)))
    splan = _scores_plan(to_rows(q_s), to_rows(k_s), to_rows(v_s), cache_k, page_table, layer=layer, seq_g=seq_g)
    if fused:
        (oa_p, state_p), (probs, o_own, picks) = _hgrn_with_scores(hplan, splan, page_table)
    else:
        oa_p, state_p = _hgrn(hg4_p, hg_lb, g_onorm, None, layer=layer, bsz=bp, t=t, c=chunk, bd=SUBLANES,
                              valid=chunk, rows_per_step=512, hps=1, out_dtype=BF16)
        probs, o_own, picks = _samp_scores(splan, page_table)
    ob_s = _samp_pv(probs, o_own, picks, cache_v, page_table, layer=layer).reshape(bs * ts, AT_HEADS * AT_DH)

    cs = -(-ts // SUBLANES) * SUBLANES
    hg4_sp = jnp.pad(hg4_s.reshape(bs, ts, -1), ((0, 0), (0, cs - ts), (0, 0))).reshape(bs * cs, -1)
    oa_s, state_s = _hgrn(hg4_sp, hg_lb, g_onorm, s0_s, layer=layer, bsz=bs, t=cs, c=cs, bd=cs, valid=ts,
                          rows_per_step=cs, hps=HG_HEADS, out_dtype=F32)
    oa_s = oa_s.reshape(bs, cs, -1)[:, :ts].reshape(bs * ts, -1)

    ob_p = _attn_prompt(q_p, k_p, vt_p, bsz=bp, t=t)
    from_t = lambda a: jnp.transpose(a.reshape(bp, AT_HEADS, AT_DH, t), (0, 3, 1, 2))
    tail = (g_post1, g_pre2, g_post2, wa, wb, wo, wgu, wd)
    xp = _mix_ffn(xp, oa_p, ob_p, gates_p, mods_p[2], mods_p[4], mods_p[3], mods_p[5], *tail, tm_p)
    xs = _mix_ffn(xs, oa_s, ob_s, gates_s, mods_s[2], mods_s[4], mods_s[3], mods_s[5], *tail, tm_s)
    return (xp, xs, state_p, state_s, from_t(kt_p), from_t(vt_p),
            k_s.reshape(bs, ts, AT_HEADS, AT_DH), v_s.reshape(bs, ts, AT_HEADS, AT_DH))


def kernel(x_prompt, x_sample, c_prompt, c_sample, state_hgrn, cache_k, cache_v, page_table, hg_lb, w_ada, b_ada,
           g_pre1, g_post1, w_in, g_onorm, w_proj_a, w_proj_b, w_out, g_pre2, g_post2, w_gu, w_down):
    bp, t, d = x_prompt.shape
    bs, ts, _ = x_sample.shape
    depth = w_in.shape[0]
    past_len = page_table.shape[1] * cache_k.shape[2]
    assert past_len % MOBA_BLOCK == 0 and ts <= MOBA_BLOCK
    hgw = 2 * HG_HEADS * HG_DK + 2 * HG_HEADS * HG_DV
    atw = AT_HEADS * AT_DH

    xp = x_prompt.reshape(bp * t, d)
    xs = x_sample.reshape(bs * ts, d)
    c_all = jnp.concatenate([c_prompt, c_sample], axis=0)
    outs = [[] for _ in range(6)]
    for l in range(depth):
        mod = _ada_mod(c_all, w_ada[l], b_ada[l])
        mods_p = [m[:bp].reshape(bp, 1, d) for m in jnp.split(mod, 6, axis=-1)]
        mods_s = [jnp.repeat(m[bp:], ts, axis=0).reshape(1, bs * ts, d) for m in jnp.split(mod, 6, axis=-1)]
        _, wqk_lo = _split_weights(w_in[l][:, hgw:hgw + 2 * atw])
        weights = (hg_lb, g_pre1[l], g_post1[l], w_in[l].astype(BF16), wqk_lo, g_onorm[l],
                   w_proj_a[l].astype(BF16), w_proj_b[l].astype(BF16), w_out[l].astype(BF16),
                   g_pre2[l], g_post2[l], w_gu[l].astype(BF16), w_down[l].astype(BF16))
        xp, xs, *rest = _layer(xp, xs, mods_p, mods_s, state_hgrn[l], cache_k, cache_v, page_table, layer=l,
                               weights=weights, dims=(bp, t, bs, ts, past_len))
        for lst, val in zip(outs, rest):
            lst.append(val)
    return (xp.reshape(bp, t, d), xs.reshape(bs, ts, d)) + tuple(jnp.stack(o) for o in outs)
```

```python
import functools
import math

import jax
import jax.numpy as jnp
from jax import lax
from jax.experimental import pallas as pl
from jax.experimental.pallas import tpu as pltpu

F32 = jnp.float32
BF16 = jnp.bfloat16

HG_HEADS = 4
HG_DK = 128
HG_DV = 128
HG_CHUNK = 128
AT_HEADS = 8
AT_DH = 64
MOBA_BLOCK = 256
MOBA_TOPK = 3
ROPE_THETA = 10000.0
EPS = 1e-6
NEG = -1e30
LOG2E = math.log2(math.e)

V7X_VMEM_LIMIT_BYTES = 52 * 1024 * 1024
LANES = 128
SUBLANES = 8
HEADS_PER_VREG = LANES // AT_DH
MXU_WIDTH = 256
PAGES_PER_STEP = 16


def _cparams(sem):
    return pltpu.CompilerParams(dimension_semantics=sem, vmem_limit_bytes=V7X_VMEM_LIMIT_BYTES)


def _dot(a, b):
    return jnp.dot(a.astype(BF16), b.astype(BF16), preferred_element_type=F32)


def _dot_nt(a, b):
    return lax.dot_general(a.astype(BF16), b.astype(BF16), (((1,), (1,)), ((), ())), preferred_element_type=F32)


def _dot_tn(a, b):
    return lax.dot_general(a.astype(BF16), b.astype(BF16), (((0,), (0,)), ((), ())), preferred_element_type=F32)


def _split2(a):
    hi = a.astype(BF16)
    lo = (a - hi.astype(F32)).astype(BF16)
    return hi, lo


def _split3(a):
    p1 = a.astype(BF16)
    r1 = a - p1.astype(F32)
    p2 = r1.astype(BF16)
    p3 = (r1 - p2.astype(F32)).astype(BF16)
    return p1, p2, p3


def _dot3(a, b):
    ah, al = _split2(a)
    bh, bl = _split2(b)
    return _dot(ah, bh) + _dot(ah, bl) + _dot(al, bh)


def _dot3_nt(a, b):
    ah, al = _split2(a)
    bh, bl = _split2(b)
    return _dot_nt(ah, bh) + _dot_nt(ah, bl) + _dot_nt(al, bh)


def _sigmoid(x):
    return 1.0 / (1.0 + jnp.exp(-x))


def _rms(x, w):
    return x * lax.rsqrt(jnp.mean(x * x, axis=-1, keepdims=True) + EPS) * w


def _ada_kernel(c_ref, w_ref, b_ref, o_ref):
    c = c_ref[...]
    o_ref[...] = _dot3(c * _sigmoid(c), w_ref[...]) + b_ref[...]


def _ada_mod(c_all, w_ada, b_ada):
    rows, d = c_all.shape
    n = w_ada.shape[1]
    tn = 1024
    return pl.pallas_call(
        _ada_kernel,
        grid=(n // tn,),
        in_specs=[
            pl.BlockSpec((rows, d), lambda j: (0, 0)),
            pl.BlockSpec((d, tn), lambda j: (0, j)),
            pl.BlockSpec((1, tn), lambda j: (0, j)),
        ],
        out_specs=pl.BlockSpec((rows, tn), lambda j: (0, j)),
        out_shape=jax.ShapeDtypeStruct((rows, n), F32),
        compiler_params=_cparams(("arbitrary",)),
        name="ada_mod",
    )(c_all, w_ada, b_ada.reshape(1, n))


def _split_kernel(w_ref, hi_ref, lo_ref):
    hi, lo = _split2(w_ref[...])
    hi_ref[...] = hi
    lo_ref[...] = lo


def _split_weights(w):
    r, c = w.shape
    tr = 256
    return pl.pallas_call(
        _split_kernel,
        grid=(r // tr,),
        in_specs=[pl.BlockSpec((tr, c), lambda i: (i, 0))],
        out_specs=[pl.BlockSpec((tr, c), lambda i: (i, 0))] * 2,
        out_shape=[jax.ShapeDtypeStruct((r, c), BF16)] * 2,
        compiler_params=_cparams(("arbitrary",)),
        name="split_weights",
    )(w)


def _rope_group(y, cos, sin_signed):
    lane = lax.broadcasted_iota(jnp.int32, (1, LANES), 1)
    first_half = (lane % AT_DH) < (AT_DH // 2)
    rot = jnp.where(first_half, pltpu.roll(y, LANES - AT_DH // 2, 1), pltpu.roll(y, AT_DH // 2, 1))
    return y * cos + rot * sin_signed


def _in_proj_kernel(x_ref, sc_ref, sh_ref, g_ref, cos_ref, sin_ref, w_ref, wlo_ref, *outs, hgw, atw, kv_t):
    if kv_t:
        hg_ref, q_ref, k_ref, kt_ref, vt_ref, gate_ref = outs
    else:
        hg_ref, q_ref, k_ref, v_ref, gate_ref = outs
    x = x_ref[...]
    h = _rms(x, g_ref[...]) * (1.0 + sc_ref[0]) + sh_ref[0]
    hh, hl = _split2(h)
    seg = 512
    for j in range(hgw // seg):
        hg_ref[:, j * seg:(j + 1) * seg] = _dot(hh, w_ref[:, j * seg:(j + 1) * seg])
    tm = x.shape[0]
    hs = jnp.concatenate([hh, hl], axis=0)
    for idx, out in enumerate((q_ref, k_ref)):
        c0 = hgw + idx * atw
        a = jnp.dot(hs, w_ref[:, c0:c0 + atw], preferred_element_type=F32)
        y = a[:tm] + a[tm:] + _dot(hh, wlo_ref[:, idx * atw:(idx + 1) * atw])
        for g in range(atw // LANES):
            yg = _rope_group(y[:, g * LANES:(g + 1) * LANES], cos_ref[...], sin_ref[...])
            out[:, g * LANES:(g + 1) * LANES] = yg
            if kv_t and idx == 1:
                kt_ref[0, g * LANES:(g + 1) * LANES, :] = yg.T
    c0 = hgw + 2 * atw
    v = _dot(hh, w_ref[:, c0:c0 + atw])
    if kv_t:
        for g in range(atw // LANES):
            vt_ref[0, g * LANES:(g + 1) * LANES, :] = v[:, g * LANES:(g + 1) * LANES].T
    else:
        v_ref[...] = v
    c0 = hgw + 3 * atw
    gw = gate_ref.shape[1]
    for j in range(gw // seg):
        gate = _sigmoid(_dot(hh, w_ref[:, c0 + j * seg:c0 + (j + 1) * seg]))
        gate_ref[:, j * seg:(j + 1) * seg] = gate.astype(gate_ref.dtype)


def _in_proj(x, sc, sh, g_pre, cos_t, sin_t, w_hi, wqk_lo, tm, seq_len=None):
    n, d = x.shape
    kv_t = seq_len is not None
    hgw = 2 * HG_HEADS * HG_DK + 2 * HG_HEADS * HG_DV
    atw = AT_HEADS * AT_DH
    gw = 2 * d
    ncols = w_hi.shape[1]
    assert ncols == hgw + 3 * atw + gw
    nt = n // tm
    nbm, r, _ = sc.shape
    tps = nt // nbm
    ctiles = cos_t.shape[0] // tm
    resident = dict(pipeline_mode=pl.Buffered(1))
    rowspec = lambda w: pl.BlockSpec((tm, w), lambda i: (i, 0))
    rowshape = lambda w: jax.ShapeDtypeStruct((n, w), F32)
    if kv_t:
        tiles_per_seq = seq_len // tm
        tspec = pl.BlockSpec((1, atw, tm), lambda i: (i // tiles_per_seq, 0, i % tiles_per_seq))
        tshape = jax.ShapeDtypeStruct((n // seq_len, atw, seq_len), F32)
        kv_specs, kv_shapes = [rowspec(atw), tspec, tspec], [rowshape(atw), tshape, tshape]
    else:
        kv_specs, kv_shapes = [rowspec(atw), rowspec(atw)], [rowshape(atw), rowshape(atw)]
    return pl.pallas_call(
        functools.partial(_in_proj_kernel, hgw=hgw, atw=atw, kv_t=kv_t),
        grid=(nt,),
        in_specs=[
            pl.BlockSpec((tm, d), lambda i: (i, 0)),
            pl.BlockSpec((1, r, d), lambda i: (i // tps, 0, 0)),
            pl.BlockSpec((1, r, d), lambda i: (i // tps, 0, 0)),
            pl.BlockSpec((1, d), lambda i: (0, 0)),
            pl.BlockSpec((tm, LANES), lambda i: (i % ctiles, 0)),
            pl.BlockSpec((tm, LANES), lambda i: (i % ctiles, 0)),
            pl.BlockSpec((d, ncols), lambda i: (0, 0), **resident),
            pl.BlockSpec((d, 2 * atw), lambda i: (0, 0), **resident),
        ],
        out_specs=[rowspec(hgw), rowspec(atw)] + kv_specs + [rowspec(gw)],
        out_shape=[rowshape(hgw), rowshape(atw)] + kv_shapes + [jax.ShapeDtypeStruct((n, gw), BF16)],
        compiler_params=_cparams(("arbitrary",)),
        name="in_proj",
    )(x, sc, sh, g_pre.reshape(1, d), cos_t, sin_t, w_hi, wqk_lo)


def _cumsum_rows(g):
    c = g.shape[0]
    row = lax.broadcasted_iota(jnp.int32, (c, c), 0)
    col = lax.broadcasted_iota(jnp.int32, (c, c), 1)
    tri = jnp.where(row >= col, 1.0, 0.0).astype(BF16)
    p1, p2, p3 = _split3(g)
    return (jnp.dot(tri, p1, preferred_element_type=F32) + jnp.dot(tri, p2, preferred_element_type=F32)
            + jnp.dot(tri, p3, preferred_element_type=F32))


def _hgrn_chunk(q, hf, v, hg, lb, gon, st, *, c, bd, valid):
    f = lb + (1.0 - lb) * _sigmoid(hf)
    gl = jnp.log(f)
    kin = 1.0 - f
    row = lax.broadcasted_iota(jnp.int32, (c, 1), 0)
    if valid < c:
        live = row < valid
        gl = jnp.where(live, gl, 0.0)
        kin = jnp.where(live, kin, 0.0)
    b = _cumsum_rows(gl)
    o = _dot_nt(q * jnp.exp(b), st)
    a = None
    m = c // 2
    while m >= bd:
        span = 2 * m
        right = (row % span) >= m
        npar = c // span
        ref = b[m - 1:m]
        for p in range(1, npar):
            ref = jnp.where(row // span == p, b[p * span + m - 1:p * span + m], ref)
        eq = jnp.exp(jnp.where(right, b - ref, NEG))
        ek = jnp.exp(jnp.where(right, NEG, ref - b))
        al = _dot_nt(q * eq, kin * ek)
        if npar > 1:
            rp = lax.broadcasted_iota(jnp.int32, (c, c), 0) // span
            cp = lax.broadcasted_iota(jnp.int32, (c, c), 1) // span
            al = jnp.where(rp == cp, al, 0.0)
        a = al if a is None else a + al
        m //= 2
    if a is not None:
        o = o + _dot(a, v)
    trow = lax.broadcasted_iota(jnp.int32, (bd, 1), 0)
    blocks = []
    for i in range(c // bd):
        sl = slice(i * bd, (i + 1) * bd)
        bi, qi, ki, vi = b[sl], q[sl], kin[sl], v[sl]
        od = jnp.zeros((bd, v.shape[1]), F32)
        for s in range(bd):
            e = jnp.exp(jnp.where(trow >= s, bi - bi[s:s + 1], NEG))
            ac = jnp.sum(qi * e * ki[s:s + 1], axis=-1, keepdims=True)
            od = od + ac * vi[s:s + 1]
        blocks.append(od)
    o = o + (blocks[0] if len(blocks) == 1 else jnp.concatenate(blocks, axis=0))
    bl = b[c - 1:c]
    st_new = st * jnp.exp(bl) + _dot_tn(v, kin * jnp.exp(bl - b))
    y = _rms(o, gon) * (hg * _sigmoid(hg))
    return y, st_new


def _hgrn_kernel(*refs, **cfg):
    _hgrn_body(refs, pl.program_id(2), pl.num_programs(2), **cfg)


def _hgrn_body(refs, cb, ncb, *, layer, c, bd, valid, n_inner, use_s0, hps):
    if use_s0:
        lb_ref, gon_ref, q_ref, f_ref, i_ref, g_ref, s0_ref, o_ref, sout_ref, st_scr = refs
    else:
        lb_ref, gon_ref, q_ref, f_ref, i_ref, g_ref, o_ref, sout_ref, st_scr = refs

    @pl.when(cb == 0)
    def _init():
        for hh in range(hps):
            st_scr[hh] = s0_ref[0, hh].T if use_s0 else jnp.zeros(st_scr.shape[1:], F32)

    rows = [lb_ref[j:j + 1, :] for j in range(lb_ref.shape[0])]
    mx = rows[0]
    for r in rows[1:]:
        mx = jnp.maximum(mx, r)
    es = [jnp.exp(r - mx) for r in rows]
    tot = es[0]
    for e in es[1:]:
        tot = tot + e
    part = es[0]
    for e in es[1:layer + 1]:
        part = part + e
    lb_all = part / tot
    gon = gon_ref[...]

    for hh in range(hps):
        ln = slice(hh * HG_DK, (hh + 1) * HG_DK)
        lb = lb_all[:, ln]
        st = st_scr[hh]
        for j in range(n_inner):
            sl = slice(j * c, (j + 1) * c)
            y, st = _hgrn_chunk(q_ref[sl, ln], f_ref[sl, ln], i_ref[sl, ln], g_ref[sl, ln], lb, gon, st,
                                c=c, bd=bd, valid=valid)
            o_ref[sl, ln] = y.astype(o_ref.dtype)
        st_scr[hh] = st

    @pl.when(cb == ncb - 1)
    def _fin():
        for hh in range(hps):
            sout_ref[0, hh] = st_scr[hh].T


def _hgrn_scores_kernel(*refs, n_hg_in, npg, hgrn_cfg, pages_per_blk):
    ins = refs[1:]
    hg_in, sc_q, kpages = ins[:n_hg_in], ins[n_hg_in:n_hg_in + 4], ins[n_hg_in + 4:n_hg_in + 4 + npg]
    rest = ins[n_hg_in + 4 + npg:]
    hg_out, sc_out, st_scr, s_scr, ks_scr = rest[:2], rest[2:5], rest[5], rest[6], rest[7]
    _hgrn_body(tuple(hg_in) + tuple(hg_out) + (st_scr,), pl.program_id(2), pl.num_programs(2), **hgrn_cfg)
    lin = (pl.program_id(0) * pl.num_programs(1) + pl.program_id(1)) * pl.num_programs(2) + pl.program_id(2)
    ng = s_scr.shape[1] // (npg * kpages[0].shape[4])
    _scores_body(sc_q, kpages, tuple(sc_out) + (s_scr, ks_scr), lin % ng, ng, pages_per_blk=pages_per_blk)


def _hgrn(hg4, hg_lb, g_onorm, s0, *, layer, bsz, t, c, bd, valid, rows_per_step, hps, out_dtype):
    plan = _hgrn_plan(hg4, hg_lb, g_onorm, s0, layer=layer, bsz=bsz, t=t, c=c, bd=bd, valid=valid,
                      rows_per_step=rows_per_step, hps=hps, out_dtype=out_dtype)
    return pl.pallas_call(
        functools.partial(_hgrn_kernel, **plan["cfg"]),
        grid=plan["grid"],
        in_specs=plan["in_specs"],
        out_specs=plan["out_specs"],
        out_shape=plan["out_shape"],
        scratch_shapes=plan["scratch"],
        compiler_params=_cparams(("arbitrary", "arbitrary", "arbitrary")),
        name="hgrn2",
    )(*plan["args"])


def _hgrn_plan(hg4, hg_lb, g_onorm, s0, *, layer, bsz, t, c, bd, valid, rows_per_step, hps, out_dtype):
    n = hg4.shape[0]
    h = HG_HEADS
    hgroups = h // hps
    steps = t // rows_per_step
    use_s0 = s0 is not None

    def col(k):
        return pl.BlockSpec((rows_per_step, hps * HG_DK),
                            lambda b, hg, cb, *_, k=k: (b * steps + cb, k * hgroups + hg))

    state_spec = pl.BlockSpec((1, hps, HG_DK, HG_DV), lambda b, hg, cb, *_: (b, hg, 0, 0))
    in_specs = [
        pl.BlockSpec((hg_lb.shape[0], hps * HG_DK), lambda b, hg, cb, *_: (0, hg)),
        pl.BlockSpec((1, HG_DV), lambda b, hg, cb, *_: (0, 0)),
        col(0), col(1), col(2), col(3),
    ]
    args = [hg_lb, g_onorm.reshape(1, HG_DV), hg4, hg4, hg4, hg4]
    if use_s0:
        in_specs.append(state_spec)
        args.append(s0)
    return dict(
        cfg=dict(layer=layer, c=c, bd=bd, valid=valid, n_inner=rows_per_step // c, use_s0=use_s0, hps=hps),
        grid=(bsz, hgroups, steps), in_specs=in_specs, args=args,
        out_specs=[pl.BlockSpec((rows_per_step, hps * HG_DV), lambda b, hg, cb, *_: (b * steps + cb, hg)),
                   state_spec],
        out_shape=[jax.ShapeDtypeStruct((n, h * HG_DV), out_dtype),
                   jax.ShapeDtypeStruct((bsz, h, HG_DK, HG_DV), F32)],
        scratch=[pltpu.VMEM((hps, HG_DV, HG_DK), F32)],
    )


def _topk_select_t(g, own, nblk):
    rown = lax.broadcasted_iota(jnp.int32, (g.shape[0], 1), 0)
    past = rown < own
    gm = jnp.where(past, g, NEG)
    rank = jnp.zeros(g.shape, F32)
    for m in range(nblk - 1):
        grow = gm[m:m + 1, :]
        tie = jnp.where(rown > m, 1.0, 0.0)
        rank = rank + jnp.where(grow > gm, 1.0, 0.0) + jnp.where(grow == gm, tie, 0.0)
    return jnp.where(past, jnp.where(rank < MOBA_TOPK, 1.0, 0.0), 0.0)


def _attn_kernel(q_ref, k_ref, vt_ref, o_ref, km_scr, qh_scr, sel_scr, s_scr, acc_scr, *, nblk):
    tq = MOBA_BLOCK
    own = pl.program_id(1)
    scale = AT_DH ** -0.5
    ngrp = AT_HEADS // HEADS_PER_VREG

    @pl.when(own == 0)
    def _means():
        km_scr[...] = jnp.zeros_like(km_scr)
        for n in range(nblk):
            blk = k_ref[n * MOBA_BLOCK:(n + 1) * MOBA_BLOCK, :]
            km_scr[n:n + 1, :] = jnp.sum(blk, axis=0, keepdims=True) * (1.0 / MOBA_BLOCK)

    lane = lax.broadcasted_iota(jnp.int32, (1, LANES), 1)
    rown = lax.broadcasted_iota(jnp.int32, (km_scr.shape[0], 1), 0)
    causal = (lax.broadcasted_iota(jnp.int32, (MOBA_BLOCK, tq), 0)
              <= lax.broadcasted_iota(jnp.int32, (MOBA_BLOCK, tq), 1))
    r_own = pl.multiple_of(own * MOBA_BLOCK, MOBA_BLOCK)

    ms = []
    for grp in range(ngrp):
        lo = grp * LANES
        qp = q_ref[:, lo:lo + LANES]
        kmp = km_scr[:, lo:lo + LANES]
        kb = k_ref[pl.ds(r_own, MOBA_BLOCK), lo:lo + LANES].astype(BF16)
        for sub in range(HEADS_PER_VREG):
            h = grp * HEADS_PER_VREG + sub
            qh = jnp.where((lane // AT_DH) == sub, qp, 0.0) * scale
            sel_scr[h] = _topk_select_t(_dot3_nt(kmp, qh), own, nblk)
            qhb = (qh * LOG2E).astype(BF16)
            qh_scr[h] = qhb
            st = jnp.where(causal, _dot_nt(kb, qhb), NEG)
            s_scr[h, own] = st
            ms.append(jnp.max(st, axis=0, keepdims=True))

    def pass_a(n, ms):
        rows = pl.ds(pl.multiple_of(n * MOBA_BLOCK, MOBA_BLOCK), MOBA_BLOCK)
        out = []
        for grp in range(ngrp):
            kb = k_ref[rows, grp * LANES:(grp + 1) * LANES].astype(BF16)
            for sub in range(HEADS_PER_VREG):
                h = grp * HEADS_PER_VREG + sub
                selrow = jnp.max(jnp.where(rown == n, sel_scr[h], 0.0), axis=0, keepdims=True)
                st = jnp.where(selrow > 0.0, _dot_nt(kb, qh_scr[h]), NEG)
                s_scr[h, n] = st
                out.append(jnp.maximum(ms[h], jnp.max(st, axis=0, keepdims=True)))
        return tuple(out)

    def blocks(step, first, count, carry, unroll):
        while unroll >= 1:
            def body(i, c, first=first, unroll=unroll):
                for u in range(unroll):
                    c = step(first + unroll * i + u, c)
                return c
            trips = count // unroll
            carry = lax.fori_loop(0, trips, body, carry)
            first, count, unroll = first + trips * unroll, count - trips * unroll, unroll // 2
        return carry

    ms = blocks(pass_a, 0, own, tuple(ms), 4)

    acc_scr[...] = jnp.zeros_like(acc_scr)
    ones = jnp.ones((acc_scr.shape[1] - AT_DH, MOBA_BLOCK), BF16)

    def pass_b(n, carry):
        cols = pl.ds(pl.multiple_of(n * MOBA_BLOCK, MOBA_BLOCK), MOBA_BLOCK)
        for h in range(AT_HEADS):
            p = jnp.exp2(s_scr[h, n] - ms[h]).astype(BF16)
            vte = jnp.concatenate([vt_ref[0, h * AT_DH:(h + 1) * AT_DH, cols].astype(BF16), ones], axis=0)
            acc_scr[h] = acc_scr[h] + jnp.dot(vte, p, preferred_element_type=F32)
        return carry

    blocks(pass_b, 0, own + 1, 0, 4)
    for grp in range(ngrp):
        parts = []
        for h in range(grp * HEADS_PER_VREG, (grp + 1) * HEADS_PER_VREG):
            acc = acc_scr[h]
            parts.append(acc[:AT_DH] / acc[AT_DH:AT_DH + 1])
        ot = jnp.concatenate(parts, axis=0)
        o_ref[:, grp * LANES:(grp + 1) * LANES] = ot.T.astype(o_ref.dtype)


def _attn_prompt(q, k, vt, *, bsz, t):
    n, w = q.shape
    nblk = t // MOBA_BLOCK
    assert t % MOBA_BLOCK == 0
    rpad = -(-nblk // SUBLANES) * SUBLANES
    return pl.pallas_call(
        functools.partial(_attn_kernel, nblk=nblk),
        grid=(bsz, nblk),
        in_specs=[
            pl.BlockSpec((MOBA_BLOCK, w), lambda b, i: (b * nblk + i, 0)),
            pl.BlockSpec((t, w), lambda b, i: (b, 0)),
            pl.BlockSpec((1, w, t), lambda b, i: (b, 0, 0)),
        ],
        out_specs=pl.BlockSpec((MOBA_BLOCK, w), lambda b, i: (b * nblk + i, 0)),
        out_shape=jax.ShapeDtypeStruct((n, w), BF16),
        scratch_shapes=[pltpu.VMEM((rpad, w), F32),
                        pltpu.VMEM((AT_HEADS, MOBA_BLOCK, LANES), BF16),
                        pltpu.VMEM((AT_HEADS, rpad, MOBA_BLOCK), F32),
                        pltpu.VMEM((AT_HEADS, nblk, MOBA_BLOCK, MOBA_BLOCK), F32),
                        pltpu.VMEM((AT_HEADS, AT_DH + 2 * SUBLANES, MOBA_BLOCK), F32)],
        compiler_params=_cparams(("arbitrary", "arbitrary")),
        name="moba_prompt",
    )(q, k, vt)


def _samp_scores_kernel(*refs, npg, pages_per_blk):
    _scores_body(refs[1:5], refs[5:5 + npg], refs[5 + npg:], pl.program_id(1), pl.num_programs(1),
                 pages_per_blk=pages_per_blk)


def _scores_body(q_refs, kpages, out_refs, g, ng, *, pages_per_blk):
    qbd_ref, qm_ref, kn_ref, vn_ref = q_refs
    p_ref, oown_ref, pk_ref, s_scr, ks_scr = out_refs
    npg = len(kpages)
    nh, dh, ps = kpages[0].shape[2:]
    rows = qm_ref.shape[1]
    scale = dh ** -0.5
    lane = lax.broadcasted_iota(jnp.int32, (1, LANES), 1)

    @pl.when(g == 0)
    def _zero():
        ks_scr[...] = jnp.zeros_like(ks_scr)

    qb = qbd_ref[0] * scale
    qh = qb.astype(BF16)
    for i in range(npg // pages_per_blk):
        pages = [kpages[i * pages_per_blk + u][0, 0].reshape(nh * dh, ps) for u in range(pages_per_blk)]
        n = g * (npg // pages_per_blk) + i
        tot = pages[0]
        for pg in pages[1:]:
            tot = tot + pg
        ks_scr[...] = jnp.where(lane == n, jnp.sum(tot, axis=-1, keepdims=True), ks_scr[...])
        kp = pages[0] if pages_per_blk == 1 else jnp.concatenate(pages, axis=1)
        s_scr[:, pl.ds(pl.multiple_of(n * pages_per_blk * ps, ps), pages_per_blk * ps)] = jnp.dot(
            qh, kp.astype(BF16), preferred_element_type=F32)

    @pl.when(g == ng - 1)
    def _finish():
        total = s_scr.shape[1]
        bw = pages_per_blk * ps
        nblk = total // bw
        gate = jnp.where(lane < nblk, _dot3(qb, ks_scr[...] * (1.0 / bw)), NEG)
        lanef = lane.astype(F32)
        cur = gate
        picks = []
        for _ in range(min(MOBA_TOPK, nblk)):
            mx = jnp.max(cur, axis=-1, keepdims=True)
            idx = jnp.min(jnp.where(cur == mx, lanef, float(LANES)), axis=-1, keepdims=True)
            picks.append(idx)
            cur = jnp.where(lanef == idx, NEG, cur)
        pk = jnp.zeros((rows, LANES), F32)
        for j, idx in enumerate(picks):
            pk = jnp.where(lane == j, idx, pk)
        pk_ref[0] = pk.astype(jnp.int32)
        qm = qm_ref[0] * scale
        so = _dot_nt(qm, kn_ref[0])
        ro = lax.broadcasted_iota(jnp.int32, (rows, 1), 0)
        co = lax.broadcasted_iota(jnp.int32, (1, rows), 1)
        own_ok = jnp.logical_and(ro % nh == co % nh, co // nh <= ro // nh)
        so = jnp.where(own_ok, so, NEG)
        m = jnp.max(so, axis=-1, keepdims=True)
        lchunk = 8 * bw if total % (8 * bw) == 0 else bw
        nch = total // lchunk
        for cidx in range(nch):
            cs = slice(cidx * lchunk, (cidx + 1) * lchunk)
            bid = (cidx * (lchunk // bw) + lax.broadcasted_iota(jnp.int32, (1, lchunk), 1) // bw).astype(F32)
            hit = jnp.zeros((rows, lchunk), F32)
            for idx in picks:
                hit = jnp.where(bid == idx, 1.0, hit)
            sc = jnp.where(hit > 0.0, s_scr[:, cs], NEG)
            s_scr[:, cs] = sc
            m = jnp.maximum(m, jnp.max(sc, axis=-1, keepdims=True))
        po = jnp.exp(so - m)
        l = jnp.sum(po, axis=-1, keepdims=True)
        for cidx in range(nch):
            cs = slice(cidx * lchunk, (cidx + 1) * lchunk)
            pc = jnp.exp(s_scr[:, cs] - m)
            s_scr[:, cs] = pc
            l = l + jnp.sum(pc, axis=-1, keepdims=True)
        inv = 1.0 / l
        for cidx in range(nch):
            cs = slice(cidx * lchunk, (cidx + 1) * lchunk)
            p_ref[0, :, cs] = s_scr[:, cs] * inv
        oown_ref[0] = _dot(po * inv, vn_ref[0])


def _samp_pv_kernel(pt_ref, pk_ref, p_ref, oown_ref, cv_hbm, o_ref, vbuf, sem, *, layer, topk, pages_per_blk):
    b = pl.program_id(0)
    rows = p_ref.shape[1]
    nh, dh, ps = cv_hbm.shape[2:]

    def slice_copy(r, j, pg):
        page = pt_ref[b, pk_ref[b, r * topk + j] * pages_per_blk + pg]
        return pltpu.make_async_copy(cv_hbm.at[layer, page, r % nh], vbuf.at[r, j * pages_per_blk + pg], sem.at[r])

    slots = [(j, pg) for j in range(topk) for pg in range(pages_per_blk)]
    for r in range(rows):
        for j, pg in slots:
            slice_copy(r, j, pg).start()

    lane = lax.broadcasted_iota(jnp.int32, (1, LANES), 1)
    ot = jnp.zeros((dh, LANES), F32)
    for r in range(rows):
        for j, pg in slots:
            pltpu.make_async_copy(cv_hbm.at[layer, 0, 0], vbuf.at[r, j * pages_per_blk + pg], sem.at[r]).wait()
        acc = jnp.zeros((dh, ps), F32)
        for j, pg in slots:
            off = pl.multiple_of((pk_ref[b, r * topk + j] * pages_per_blk + pg) * ps, ps)
            acc = acc + vbuf[r, j * pages_per_blk + pg] * p_ref[0, r:r + 1, pl.ds(off, ps)]
        ot = jnp.where(lane == r, jnp.sum(acc, axis=-1, keepdims=True), ot)
    o_ref[0] = oown_ref[0] + ot.T[:rows, :]


def _scores_plan(qm, kn, vn, cache_k, page_table, *, layer, seq_g):
    bs, rows, dh = qm.shape
    _, _, ps, nh, _ = cache_k.shape
    npages = page_table.shape[1]
    pages_per_blk = MOBA_BLOCK // ps
    assert MOBA_BLOCK % ps == 0 and npages % pages_per_blk == 0 and npages // pages_per_blk <= LANES
    assert rows <= LANES
    npg = math.gcd(npages, PAGES_PER_STEP)
    assert npg % pages_per_blk == 0
    total = npages * ps
    ck = jnp.transpose(cache_k, (0, 1, 3, 4, 2))
    head_of_row = jnp.arange(rows, dtype=jnp.int32) % nh
    head_of_col = jnp.arange(nh * dh, dtype=jnp.int32) // dh
    qbd = jnp.where(head_of_row[:, None] == head_of_col[None, :], jnp.tile(qm, (1, 1, nh)), 0.0)

    def per_seq(width):
        return pl.BlockSpec((1, rows, width), lambda *idx: (seq_g(*idx[:-1])[0], 0, 0))

    def page_spec(j):
        def index_map(*idx):
            seq, g = seq_g(*idx[:-1])
            return (layer, idx[-1][seq, g * npg + j], 0, 0, 0)
        return pl.BlockSpec((1, 1, nh, dh, ps), index_map)

    return dict(
        npg=npg, pages_per_blk=pages_per_blk, groups=npages // npg, bs=bs,
        in_specs=[per_seq(nh * dh), per_seq(dh), per_seq(dh), per_seq(dh)] + [page_spec(j) for j in range(npg)],
        args=[qbd, qm, kn, vn] + [ck] * npg,
        out_specs=[per_seq(total), per_seq(dh), per_seq(LANES)],
        out_shape=[jax.ShapeDtypeStruct((bs, rows, total), F32), jax.ShapeDtypeStruct((bs, rows, dh), F32),
                   jax.ShapeDtypeStruct((bs, rows, LANES), jnp.int32)],
        scratch=[pltpu.VMEM((rows, total), F32), pltpu.VMEM((nh * dh, LANES), F32)],
    )


def _samp_scores(plan, page_table):
    return pl.pallas_call(
        functools.partial(_samp_scores_kernel, npg=plan["npg"], pages_per_blk=plan["pages_per_blk"]),
        grid_spec=pltpu.PrefetchScalarGridSpec(
            num_scalar_prefetch=1, grid=(plan["bs"], plan["groups"]),
            in_specs=plan["in_specs"], out_specs=plan["out_specs"], scratch_shapes=plan["scratch"]),
        out_shape=plan["out_shape"],
        compiler_params=_cparams(("arbitrary", "arbitrary")),
        name="moba_sample_scores",
    )(page_table, *plan["args"])


def _hgrn_with_scores(hplan, splan, page_table):
    outs = pl.pallas_call(
        functools.partial(_hgrn_scores_kernel, n_hg_in=len(hplan["in_specs"]), npg=splan["npg"],
                          hgrn_cfg=hplan["cfg"], pages_per_blk=splan["pages_per_blk"]),
        grid_spec=pltpu.PrefetchScalarGridSpec(
            num_scalar_prefetch=1, grid=hplan["grid"],
            in_specs=hplan["in_specs"] + splan["in_specs"],
            out_specs=hplan["out_specs"] + splan["out_specs"],
            scratch_shapes=hplan["scratch"] + splan["scratch"]),
        out_shape=hplan["out_shape"] + splan["out_shape"],
        compiler_params=_cparams(("arbitrary", "arbitrary", "arbitrary")),
        name="hgrn2_with_sample_scores",
    )(page_table, *hplan["args"], *splan["args"])
    return outs[:2], outs[2:]


def _samp_pv(probs, o_own, picks, cache_v, page_table, *, layer):
    bs, rows, total = probs.shape
    _, _, ps, nh, dh = cache_v.shape
    pages_per_blk = MOBA_BLOCK // ps
    topk = min(MOBA_TOPK, page_table.shape[1] // pages_per_blk)
    cv = jnp.transpose(cache_v, (0, 1, 3, 4, 2))
    picks = picks[:, :, :topk].reshape(bs, rows * topk)
    seq_spec = lambda w: pl.BlockSpec((1, rows, w), lambda b, pt, pk: (b, 0, 0))
    return pl.pallas_call(
        functools.partial(_samp_pv_kernel, layer=layer, topk=topk, pages_per_blk=pages_per_blk),
        grid_spec=pltpu.PrefetchScalarGridSpec(
            num_scalar_prefetch=2,
            grid=(bs,),
            in_specs=[seq_spec(total), seq_spec(dh), pl.BlockSpec(memory_space=pl.ANY)],
            out_specs=seq_spec(dh),
            scratch_shapes=[pltpu.VMEM((rows, topk * pages_per_blk, dh, ps), F32), pltpu.SemaphoreType.DMA((rows,))],
        ),
        out_shape=jax.ShapeDtypeStruct((bs, rows, dh), F32),
        compiler_params=_cparams(("arbitrary",)),
        name="moba_sample_pv",
    )(page_table, picks, probs, o_own, cv)


def _mix_ffn_kernel(x_ref, oa_ref, ob_ref, gate_ref, gt1_ref, sc2_ref, sh2_ref, gt2_ref,
                    gpost1_ref, gpre2_ref, gpost2_ref, wa_ref, wb_ref, wo_ref, wgu_ref, wd_ref, out_ref,
                    *, dff, fchunk):
    d = x_ref.shape[1]
    merged = gate_ref[:, :d] * _dot(oa_ref[...], wa_ref[...]) + gate_ref[:, d:] * _dot(ob_ref[...], wb_ref[...])
    x1 = x_ref[...] + gt1_ref[0] * _rms(_dot(merged, wo_ref[...]), gpost1_ref[...])
    hb = (_rms(x1, gpre2_ref[...]) * (1.0 + sc2_ref[0]) + sh2_ref[0]).astype(BF16)
    acc = jnp.zeros(x1.shape, F32)
    for c0 in range(0, dff, fchunk):
        c1 = min(c0 + fchunk, dff)
        g = _dot(hb, wgu_ref[:, c0:c1])
        u = _dot(hb, wgu_ref[:, dff + c0:dff + c1])
        acc = acc + _dot(g * _sigmoid(g) * u, wd_ref[c0:c1, :])
    out_ref[...] = x1 + gt2_ref[0] * _rms(acc, gpost2_ref[...])


def _mix_ffn(x, oa, ob, gates, gt1, sc2, sh2, gt2, g_post1, g_pre2, g_post2, wa, wb, wo, wgu, wd, tm):
    n, d = x.shape
    dff = wd.shape[0]
    nt = n // tm
    nbm, r, _ = gt1.shape
    tps = nt // nbm
    fchunk = 3 * MXU_WIDTH
    resident = dict(pipeline_mode=pl.Buffered(1))
    rowspec = lambda w: pl.BlockSpec((tm, w), lambda i: (i, 0))
    modspec = pl.BlockSpec((1, r, d), lambda i: (i // tps, 0, 0))
    vecspec = pl.BlockSpec((1, d), lambda i: (0, 0))
    wspec = lambda w: pl.BlockSpec(w.shape, lambda i: (0, 0), **resident)
    return pl.pallas_call(
        functools.partial(_mix_ffn_kernel, dff=dff, fchunk=fchunk),
        grid=(nt,),
        in_specs=[rowspec(d), rowspec(oa.shape[1]), rowspec(ob.shape[1]), rowspec(gates.shape[1]),
                  modspec, modspec, modspec, modspec, vecspec, vecspec, vecspec,
                  wspec(wa), wspec(wb), wspec(wo), wspec(wgu), wspec(wd)],
        out_specs=rowspec(d),
        out_shape=jax.ShapeDtypeStruct((n, d), F32),
        compiler_params=_cparams(("arbitrary",)),
        name="merge_proj_ffn",
    )(x, oa, ob, gates, gt1, sc2, sh2, gt2, g_post1.reshape(1, d), g_pre2.reshape(1, d), g_post2.reshape(1, d),
      wa, wb, wo, wgu, wd)


def _rope_tables(pos):
    half = AT_DH // 2
    inv = ROPE_THETA ** (-jnp.arange(half, dtype=F32) / half)
    ang = pos.astype(F32)[:, None] * inv[None, :]
    cos = jnp.cos(ang)
    sin = jnp.sin(ang)
    cos_h = jnp.concatenate([cos, cos], axis=-1)
    sin_h = jnp.concatenate([-sin, sin], axis=-1)
    return jnp.tile(cos_h, (1, HEADS_PER_VREG)), jnp.tile(sin_h, (1, HEADS_PER_VREG))


def _project(x2d, mods, *, t, pos0, weights, tm, transposed_kv):
    sh1, sc1 = mods[0], mods[1]
    g_pre1, w_in_hi, wqk_lo = weights[1], weights[3], weights[4]
    cos_t, sin_t = _rope_tables(pos0 + jnp.arange(t, dtype=jnp.int32))
    if tm > t:
        cos_t = jnp.tile(cos_t, (tm // t, 1))
        sin_t = jnp.tile(sin_t, (tm // t, 1))
    return _in_proj(x2d, sc1, sh1, g_pre1, cos_t, sin_t, w_in_hi, wqk_lo, tm, seq_len=t if transposed_kv else None)


def _layer(xp, xs, mods_p, mods_s, s0_s, cache_k, cache_v, page_table, *, layer, weights, dims):
    bp, t, bs, ts, past_len = dims
    (hg_lb, g_pre1, g_post1, w_in_hi, wqk_lo, g_onorm, wa, wb, wo, g_pre2, g_post2, wgu, wd) = weights
    tm_p, tm_s = 512, bs * ts
    hg4_p, q_p, k_p, kt_p, vt_p, gates_p = _project(xp, mods_p, t=t, pos0=0, weights=weights, tm=tm_p,
                                                     transposed_kv=True)
    hg4_s, q_s, k_s, v_s, gates_s = _project(xs, mods_s, t=ts, pos0=past_len, weights=weights, tm=tm_s,
                                             transposed_kv=False)

    chunk = HG_CHUNK if t % HG_CHUNK == 0 else t
    hplan = _hgrn_plan(hg4_p, hg_lb, g_onorm, None, layer=layer, bsz=bp, t=t, c=chunk, bd=SUBLANES, valid=chunk,
                       rows_per_step=512, hps=1, out_dtype=BF16)
    to_rows = lambda a: a.reshape(bs, ts * AT_HEADS, AT_DH)
    groups = page_table.shape[1] // math.gcd(page_table.shape[1], PAGES_PER_STEP)
    _, hgroups, steps = hplan["grid"]
    fused = bp * hgroups * steps == bs * groups
    seq_g = ((lambda b, hg, cb: divmod((b * hgroups + hg) * steps + cb, groups)) if fused
             else (lambda b, g: (b, g)))
    splan = _scores_plan(to_rows(q_s), to_rows(k_s), to_rows(v_s), cache_k, page_table, layer=layer, seq_g=seq_g)
    if fused:
        (oa_p, state_p), (probs, o_own, picks) = _hgrn_with_scores(hplan, splan, page_table)
    else:
        oa_p, state_p = _hgrn(hg4_p, hg_lb, g_onorm, None, layer=layer, bsz=bp, t=t, c=chunk, bd=SUBLANES,
                              valid=chunk, rows_per_step=512, hps=1, out_dtype=BF16)
        probs, o_own, picks = _samp_scores(splan, page_table)
    ob_s = _samp_pv(probs, o_own, picks, cache_v, page_table, layer=layer).reshape(bs * ts, AT_HEADS * AT_DH)

    cs = -(-ts // SUBLANES) * SUBLANES
    hg4_sp = jnp.pad(hg4_s.reshape(bs, ts, -1), ((0, 0), (0, cs - ts), (0, 0))).reshape(bs * cs, -1)
    oa_s, state_s = _hgrn(hg4_sp, hg_lb, g_onorm, s0_s, layer=layer, bsz=bs, t=cs, c=cs, bd=cs, valid=ts,
                          rows_per_step=cs, hps=HG_HEADS, out_dtype=F32)
    oa_s = oa_s.reshape(bs, cs, -1)[:, :ts].reshape(bs * ts, -1)

    ob_p = _attn_prompt(q_p, k_p, vt_p, bsz=bp, t=t)
    from_t = lambda a: jnp.transpose(a.reshape(bp, AT_HEADS, AT_DH, t), (0, 3, 1, 2))
    tail = (g_post1, g_pre2, g_post2, wa, wb, wo, wgu, wd)
    xp = _mix_ffn(xp, oa_p, ob_p, gates_p, mods_p[2], mods_p[4], mods_p[3], mods_p[5], *tail, tm_p)
    xs = _mix_ffn(xs, oa_s, ob_s, gates_s, mods_s[2], mods_s[4], mods_s[3], mods_s[5], *tail, tm_s)
    return (xp, xs, state_p, state_s, from_t(kt_p), from_t(vt_p),
            k_s.reshape(bs, ts, AT_HEADS, AT_DH), v_s.reshape(bs, ts, AT_HEADS, AT_DH))


def kernel(x_prompt, x_sample, c_prompt, c_sample, state_hgrn, cache_k, cache_v, page_table, hg_lb, w_ada, b_ada,
           g_pre1, g_post1, w_in, g_onorm, w_proj_a, w_proj_b, w_out, g_pre2, g_post2, w_gu, w_down):
    bp, t, d = x_prompt.shape
    bs, ts, _ = x_sample.shape
    depth = w_in.shape[0]
    past_len = page_table.shape[1] * cache_k.shape[2]
    assert past_len % MOBA_BLOCK == 0 and ts <= MOBA_BLOCK
    hgw = 2 * HG_HEADS * HG_DK + 2 * HG_HEADS * HG_DV
    atw = AT_HEADS * AT_DH

    xp = x_prompt.reshape(bp * t, d)
    xs = x_sample.reshape(bs * ts, d)
    c_all = jnp.concatenate([c_prompt, c_sample], axis=0)
    outs = [[] for _ in range(6)]
    for l in range(depth):
        mod = _ada_mod(c_all, w_ada[l], b_ada[l])
        mods_p = [m[:bp].reshape(bp, 1, d) for m in jnp.split(mod, 6, axis=-1)]
        mods_s = [jnp.repeat(m[bp:], ts, axis=0).reshape(1, bs * ts, d) for m in jnp.split(mod, 6, axis=-1)]
        _, wqk_lo = _split_weights(w_in[l][:, hgw:hgw + 2 * atw])
        weights = (hg_lb, g_pre1[l], g_post1[l], w_in[l].astype(BF16), wqk_lo, g_onorm[l],
                   w_proj_a[l].astype(BF16), w_proj_b[l].astype(BF16), w_out[l].astype(BF16),
                   g_pre2[l], g_post2[l], w_gu[l].astype(BF16), w_down[l].astype(BF16))
        xp, xs, *rest = _layer(xp, xs, mods_p, mods_s, state_hgrn[l], cache_k, cache_v, page_table, layer=l,
                               weights=weights, dims=(bp, t, bs, ts, past_len))
        for lst, val in zip(outs, rest):
            lst.append(val)
    return (xp.reshape(bp, t, d), xs.reshape(bs, ts, d)) + tuple(jnp.stack(o) for o in outs)
```

```python
import functools
import math

import jax
import jax.numpy as jnp
from jax import lax
from jax.experimental import pallas as pl
from jax.experimental.pallas import tpu as pltpu

F32 = jnp.float32
BF16 = jnp.bfloat16

HG_HEADS = 4
HG_DK = 128
HG_DV = 128
HG_CHUNK = 128
AT_HEADS = 8
AT_DH = 64
MOBA_BLOCK = 256
MOBA_TOPK = 3
ROPE_THETA = 10000.0
EPS = 1e-6
NEG = -1e30
LOG2E = math.log2(math.e)

V7X_VMEM_LIMIT_BYTES = 52 * 1024 * 1024
LANES = 128
SUBLANES = 8
HEADS_PER_VREG = LANES // AT_DH
MXU_WIDTH = 256
PAGES_PER_STEP = 16


def _cparams(sem):
    return pltpu.CompilerParams(dimension_semantics=sem, vmem_limit_bytes=V7X_VMEM_LIMIT_BYTES)


def _dot(a, b):
    return jnp.dot(a.astype(BF16), b.astype(BF16), preferred_element_type=F32)


def _dot_nt(a, b):
    return lax.dot_general(a.astype(BF16), b.astype(BF16), (((1,), (1,)), ((), ())), preferred_element_type=F32)


def _dot_tn(a, b):
    return lax.dot_general(a.astype(BF16), b.astype(BF16), (((0,), (0,)), ((), ())), preferred_element_type=F32)


def _split2(a):
    hi = a.astype(BF16)
    lo = (a - hi.astype(F32)).astype(BF16)
    return hi, lo


def _split3(a):
    p1 = a.astype(BF16)
    r1 = a - p1.astype(F32)
    p2 = r1.astype(BF16)
    p3 = (r1 - p2.astype(F32)).astype(BF16)
    return p1, p2, p3


def _dot3(a, b):
    ah, al = _split2(a)
    bh, bl = _split2(b)
    return _dot(ah, bh) + _dot(ah, bl) + _dot(al, bh)


def _dot3_nt(a, b):
    ah, al = _split2(a)
    bh, bl = _split2(b)
    return _dot_nt(ah, bh) + _dot_nt(ah, bl) + _dot_nt(al, bh)


def _sigmoid(x):
    return 1.0 / (1.0 + jnp.exp(-x))


def _rms(x, w):
    return x * lax.rsqrt(jnp.mean(x * x, axis=-1, keepdims=True) + EPS) * w


def _ada_kernel(c_ref, w_ref, b_ref, o_ref):
    c = c_ref[...]
    o_ref[...] = _dot3(c * _sigmoid(c), w_ref[...]) + b_ref[...]


def _ada_mod(c_all, w_ada, b_ada):
    rows, d = c_all.shape
    n = w_ada.shape[1]
    tn = 1024
    return pl.pallas_call(
        _ada_kernel,
        grid=(n // tn,),
        in_specs=[
            pl.BlockSpec((rows, d), lambda j: (0, 0)),
            pl.BlockSpec((d, tn), lambda j: (0, j)),
            pl.BlockSpec((1, tn), lambda j: (0, j)),
        ],
        out_specs=pl.BlockSpec((rows, tn), lambda j: (0, j)),
        out_shape=jax.ShapeDtypeStruct((rows, n), F32),
        compiler_params=_cparams(("arbitrary",)),
        name="ada_mod",
    )(c_all, w_ada, b_ada.reshape(1, n))


def _split_kernel(w_ref, hi_ref, lo_ref):
    hi, lo = _split2(w_ref[...])
    hi_ref[...] = hi
    lo_ref[...] = lo


def _split_weights(w):
    r, c = w.shape
    tr = 256
    return pl.pallas_call(
        _split_kernel,
        grid=(r // tr,),
        in_specs=[pl.BlockSpec((tr, c), lambda i: (i, 0))],
        out_specs=[pl.BlockSpec((tr, c), lambda i: (i, 0))] * 2,
        out_shape=[jax.ShapeDtypeStruct((r, c), BF16)] * 2,
        compiler_params=_cparams(("arbitrary",)),
        name="split_weights",
    )(w)


def _rope_group(y, cos, sin_signed):
    lane = lax.broadcasted_iota(jnp.int32, (1, LANES), 1)
    first_half = (lane % AT_DH) < (AT_DH // 2)
    rot = jnp.where(first_half, pltpu.roll(y, LANES - AT_DH // 2, 1), pltpu.roll(y, AT_DH // 2, 1))
    return y * cos + rot * sin_signed


def _in_proj_kernel(x_ref, sc_ref, sh_ref, g_ref, cos_ref, sin_ref, w_ref, wlo_ref, *outs, hgw, atw, kv_t):
    if kv_t:
        hg_ref, q_ref, k_ref, kt_ref, vt_ref, gate_ref = outs
    else:
        hg_ref, q_ref, k_ref, v_ref, gate_ref = outs
    x = x_ref[...]
    h = _rms(x, g_ref[...]) * (1.0 + sc_ref[0]) + sh_ref[0]
    hh, hl = _split2(h)
    seg = 512
    for j in range(hgw // seg):
        hg_ref[:, j * seg:(j + 1) * seg] = _dot(hh, w_ref[:, j * seg:(j + 1) * seg])
    tm = x.shape[0]
    hs = jnp.concatenate([hh, hl], axis=0)
    for idx, out in enumerate((q_ref, k_ref)):
        c0 = hgw + idx * atw
        a = jnp.dot(hs, w_ref[:, c0:c0 + atw], preferred_element_type=F32)
        y = a[:tm] + a[tm:] + _dot(hh, wlo_ref[:, idx * atw:(idx + 1) * atw])
        for g in range(atw // LANES):
            yg = _rope_group(y[:, g * LANES:(g + 1) * LANES], cos_ref[...], sin_ref[...])
            out[:, g * LANES:(g + 1) * LANES] = yg
            if kv_t and idx == 1:
                kt_ref[0, g * LANES:(g + 1) * LANES, :] = yg.T
    c0 = hgw + 2 * atw
    v = _dot(hh, w_ref[:, c0:c0 + atw])
    if kv_t:
        for g in range(atw // LANES):
            vt_ref[0, g * LANES:(g + 1) * LANES, :] = v[:, g * LANES:(g + 1) * LANES].T
    else:
        v_ref[...] = v
    c0 = hgw + 3 * atw
    gw = gate_ref.shape[1]
    for j in range(gw // seg):
        gate = _sigmoid(_dot(hh, w_ref[:, c0 + j * seg:c0 + (j + 1) * seg]))
        gate_ref[:, j * seg:(j + 1) * seg] = gate.astype(gate_ref.dtype)


def _in_proj(x, sc, sh, g_pre, cos_t, sin_t, w_hi, wqk_lo, tm, seq_len=None):
    n, d = x.shape
    kv_t = seq_len is not None
    hgw = 2 * HG_HEADS * HG_DK + 2 * HG_HEADS * HG_DV
    atw = AT_HEADS * AT_DH
    gw = 2 * d
    ncols = w_hi.shape[1]
    assert ncols == hgw + 3 * atw + gw
    nt = n // tm
    nbm, r, _ = sc.shape
    tps = nt // nbm
    ctiles = cos_t.shape[0] // tm
    resident = dict(pipeline_mode=pl.Buffered(1))
    rowspec = lambda w: pl.BlockSpec((tm, w), lambda i: (i, 0))
    rowshape = lambda w: jax.ShapeDtypeStruct((n, w), F32)
    if kv_t:
        tiles_per_seq = seq_len // tm
        tspec = pl.BlockSpec((1, atw, tm), lambda i: (i // tiles_per_seq, 0, i % tiles_per_seq))
        tshape = jax.ShapeDtypeStruct((n // seq_len, atw, seq_len), F32)
        kv_specs, kv_shapes = [rowspec(atw), tspec, tspec], [rowshape(atw), tshape, tshape]
    else:
        kv_specs, kv_shapes = [rowspec(atw), rowspec(atw)], [rowshape(atw), rowshape(atw)]
    return pl.pallas_call(
        functools.partial(_in_proj_kernel, hgw=hgw, atw=atw, kv_t=kv_t),
        grid=(nt,),
        in_specs=[
            pl.BlockSpec((tm, d), lambda i: (i, 0)),
            pl.BlockSpec((1, r, d), lambda i: (i // tps, 0, 0)),
            pl.BlockSpec((1, r, d), lambda i: (i // tps, 0, 0)),
            pl.BlockSpec((1, d), lambda i: (0, 0)),
            pl.BlockSpec((tm, LANES), lambda i: (i % ctiles, 0)),
            pl.BlockSpec((tm, LANES), lambda i: (i % ctiles, 0)),
            pl.BlockSpec((d, ncols), lambda i: (0, 0), **resident),
            pl.BlockSpec((d, 2 * atw), lambda i: (0, 0), **resident),
        ],
        out_specs=[rowspec(hgw), rowspec(atw)] + kv_specs + [rowspec(gw)],
        out_shape=[rowshape(hgw), rowshape(atw)] + kv_shapes + [jax.ShapeDtypeStruct((n, gw), BF16)],
        compiler_params=_cparams(("arbitrary",)),
        name="in_proj",
    )(x, sc, sh, g_pre.reshape(1, d), cos_t, sin_t, w_hi, wqk_lo)


def _cumsum_rows(g):
    c = g.shape[0]
    row = lax.broadcasted_iota(jnp.int32, (c, c), 0)
    col = lax.broadcasted_iota(jnp.int32, (c, c), 1)
    tri = jnp.where(row >= col, 1.0, 0.0).astype(BF16)
    p1, p2, p3 = _split3(g)
    return (jnp.dot(tri, p1, preferred_element_type=F32) + jnp.dot(tri, p2, preferred_element_type=F32)
            + jnp.dot(tri, p3, preferred_element_type=F32))


def _hgrn_chunk(q, hf, v, hg, lb, gon, st, *, c, bd, valid):
    f = lb + (1.0 - lb) * _sigmoid(hf)
    gl = jnp.log2(f)
    kin = 1.0 - f
    row = lax.broadcasted_iota(jnp.int32, (c, 1), 0)
    if valid < c:
        live = row < valid
        gl = jnp.where(live, gl, 0.0)
        kin = jnp.where(live, kin, 0.0)
    b = _cumsum_rows(gl)
    o = _dot_nt(q * jnp.exp2(b), st)
    a = None
    m = c // 2
    while m >= bd:
        span = 2 * m
        right = (row % span) >= m
        npar = c // span
        ref = b[m - 1:m]
        for p in range(1, npar):
            ref = jnp.where(row // span == p, b[p * span + m - 1:p * span + m], ref)
        eq = jnp.exp2(jnp.where(right, b - ref, NEG))
        ek = jnp.exp2(jnp.where(right, NEG, ref - b))
        al = _dot_nt(q * eq, kin * ek)
        if npar > 1:
            rp = lax.broadcasted_iota(jnp.int32, (c, c), 0) // span
            cp = lax.broadcasted_iota(jnp.int32, (c, c), 1) // span
            al = jnp.where(rp == cp, al, 0.0)
        a = al if a is None else a + al
        m //= 2
    if a is not None:
        o = o + _dot(a, v)
    trow = lax.broadcasted_iota(jnp.int32, (bd, 1), 0)
    blocks = []
    for i in range(c // bd):
        sl = slice(i * bd, (i + 1) * bd)
        bi, qi, ki, vi = b[sl], q[sl], kin[sl], v[sl]
        od = jnp.zeros((bd, v.shape[1]), F32)
        for s in range(bd):
            d = bi - bi[s:s + 1]
            e = jnp.exp2(d if s == 0 else jnp.where(trow >= s, d, NEG))
            ac = jnp.sum(qi * e * ki[s:s + 1], axis=-1, keepdims=True)
            od = od + ac * vi[s:s + 1]
        blocks.append(od)
    o = o + (blocks[0] if len(blocks) == 1 else jnp.concatenate(blocks, axis=0))
    bl = b[c - 1:c]
    st_new = st * jnp.exp2(bl) + _dot_tn(v, kin * jnp.exp2(bl - b))
    y = _rms(o, gon) * (hg * _sigmoid(hg))
    return y, st_new


def _hgrn_kernel(*refs, **cfg):
    _hgrn_body(refs, pl.program_id(2), pl.num_programs(2), **cfg)


def _hgrn_body(refs, cb, ncb, *, layer, c, bd, valid, n_inner, use_s0, hps):
    if use_s0:
        lb_ref, gon_ref, q_ref, f_ref, i_ref, g_ref, s0_ref, o_ref, sout_ref, st_scr = refs
    else:
        lb_ref, gon_ref, q_ref, f_ref, i_ref, g_ref, o_ref, sout_ref, st_scr = refs

    @pl.when(cb == 0)
    def _init():
        for hh in range(hps):
            st_scr[hh] = s0_ref[0, hh].T if use_s0 else jnp.zeros(st_scr.shape[1:], F32)

    rows = [lb_ref[j:j + 1, :] for j in range(lb_ref.shape[0])]
    mx = rows[0]
    for r in rows[1:]:
        mx = jnp.maximum(mx, r)
    es = [jnp.exp(r - mx) for r in rows]
    tot = es[0]
    for e in es[1:]:
        tot = tot + e
    part = es[0]
    for e in es[1:layer + 1]:
        part = part + e
    lb_all = part / tot
    gon = gon_ref[...]

    for hh in range(hps):
        ln = slice(hh * HG_DK, (hh + 1) * HG_DK)
        lb = lb_all[:, ln]
        st = st_scr[hh]
        for j in range(n_inner):
            sl = slice(j * c, (j + 1) * c)
            y, st = _hgrn_chunk(q_ref[sl, ln], f_ref[sl, ln], i_ref[sl, ln], g_ref[sl, ln], lb, gon, st,
                                c=c, bd=bd, valid=valid)
            o_ref[sl, ln] = y.astype(o_ref.dtype)
        st_scr[hh] = st

    @pl.when(cb == ncb - 1)
    def _fin():
        for hh in range(hps):
            sout_ref[0, hh] = st_scr[hh].T


def _hgrn_scores_kernel(*refs, n_hg_in, npg, hgrn_cfg, pages_per_blk):
    ins = refs[1:]
    hg_in, sc_q, kpages = ins[:n_hg_in], ins[n_hg_in:n_hg_in + 4], ins[n_hg_in + 4:n_hg_in + 4 + npg]
    rest = ins[n_hg_in + 4 + npg:]
    hg_out, sc_out, st_scr, s_scr, ks_scr = rest[:2], rest[2:5], rest[5], rest[6], rest[7]
    _hgrn_body(tuple(hg_in) + tuple(hg_out) + (st_scr,), pl.program_id(2), pl.num_programs(2), **hgrn_cfg)
    lin = (pl.program_id(0) * pl.num_programs(1) + pl.program_id(1)) * pl.num_programs(2) + pl.program_id(2)
    ng = s_scr.shape[1] // (npg * kpages[0].shape[4])
    _scores_body(sc_q, kpages, tuple(sc_out) + (s_scr, ks_scr), lin % ng, ng, pages_per_blk=pages_per_blk)


def _hgrn(hg4, hg_lb, g_onorm, s0, *, layer, bsz, t, c, bd, valid, rows_per_step, hps, out_dtype):
    plan = _hgrn_plan(hg4, hg_lb, g_onorm, s0, layer=layer, bsz=bsz, t=t, c=c, bd=bd, valid=valid,
                      rows_per_step=rows_per_step, hps=hps, out_dtype=out_dtype)
    return pl.pallas_call(
        functools.partial(_hgrn_kernel, **plan["cfg"]),
        grid=plan["grid"],
        in_specs=plan["in_specs"],
        out_specs=plan["out_specs"],
        out_shape=plan["out_shape"],
        scratch_shapes=plan["scratch"],
        compiler_params=_cparams(("arbitrary", "arbitrary", "arbitrary")),
        name="hgrn2",
    )(*plan["args"])


def _hgrn_plan(hg4, hg_lb, g_onorm, s0, *, layer, bsz, t, c, bd, valid, rows_per_step, hps, out_dtype):
    n = hg4.shape[0]
    h = HG_HEADS
    hgroups = h // hps
    steps = t // rows_per_step
    use_s0 = s0 is not None

    def col(k):
        return pl.BlockSpec((rows_per_step, hps * HG_DK),
                            lambda b, hg, cb, *_, k=k: (b * steps + cb, k * hgroups + hg))

    state_spec = pl.BlockSpec((1, hps, HG_DK, HG_DV), lambda b, hg, cb, *_: (b, hg, 0, 0))
    in_specs = [
        pl.BlockSpec((hg_lb.shape[0], hps * HG_DK), lambda b, hg, cb, *_: (0, hg)),
        pl.BlockSpec((1, HG_DV), lambda b, hg, cb, *_: (0, 0)),
        col(0), col(1), col(2), col(3),
    ]
    args = [hg_lb, g_onorm.reshape(1, HG_DV), hg4, hg4, hg4, hg4]
    if use_s0:
        in_specs.append(state_spec)
        args.append(s0)
    return dict(
        cfg=dict(layer=layer, c=c, bd=bd, valid=valid, n_inner=rows_per_step // c, use_s0=use_s0, hps=hps),
        grid=(bsz, hgroups, steps), in_specs=in_specs, args=args,
        out_specs=[pl.BlockSpec((rows_per_step, hps * HG_DV), lambda b, hg, cb, *_: (b * steps + cb, hg)),
                   state_spec],
        out_shape=[jax.ShapeDtypeStruct((n, h * HG_DV), out_dtype),
                   jax.ShapeDtypeStruct((bsz, h, HG_DK, HG_DV), F32)],
        scratch=[pltpu.VMEM((hps, HG_DV, HG_DK), F32)],
    )


def _topk_select_t(g, own, nblk):
    rown = lax.broadcasted_iota(jnp.int32, (g.shape[0], 1), 0)
    past = rown < own
    gm = jnp.where(past, g, NEG)
    rank = jnp.zeros(g.shape, F32)
    for m in range(nblk - 1):
        grow = gm[m:m + 1, :]
        tie = jnp.where(rown > m, 1.0, 0.0)
        rank = rank + jnp.where(grow > gm, 1.0, 0.0) + jnp.where(grow == gm, tie, 0.0)
    return jnp.where(past, jnp.where(rank < MOBA_TOPK, 1.0, 0.0), 0.0)


def _attn_kernel(q_ref, k_ref, vt_ref, o_ref, km_scr, qh_scr, sel_scr, s_scr, acc_scr, *, nblk):
    tq = MOBA_BLOCK
    own = pl.program_id(1)
    scale = AT_DH ** -0.5
    ngrp = AT_HEADS // HEADS_PER_VREG

    @pl.when(own == 0)
    def _means():
        km_scr[...] = jnp.zeros_like(km_scr)
        for n in range(nblk):
            blk = k_ref[n * MOBA_BLOCK:(n + 1) * MOBA_BLOCK, :]
            km_scr[n:n + 1, :] = jnp.sum(blk, axis=0, keepdims=True) * (1.0 / MOBA_BLOCK)

    lane = lax.broadcasted_iota(jnp.int32, (1, LANES), 1)
    rown = lax.broadcasted_iota(jnp.int32, (km_scr.shape[0], 1), 0)
    causal = (lax.broadcasted_iota(jnp.int32, (MOBA_BLOCK, tq), 0)
              <= lax.broadcasted_iota(jnp.int32, (MOBA_BLOCK, tq), 1))
    r_own = pl.multiple_of(own * MOBA_BLOCK, MOBA_BLOCK)

    ms = []
    for grp in range(ngrp):
        lo = grp * LANES
        qp = q_ref[:, lo:lo + LANES]
        kmp = km_scr[:, lo:lo + LANES]
        kb = k_ref[pl.ds(r_own, MOBA_BLOCK), lo:lo + LANES].astype(BF16)
        for sub in range(HEADS_PER_VREG):
            h = grp * HEADS_PER_VREG + sub
            qh = jnp.where((lane // AT_DH) == sub, qp, 0.0) * scale
            sel_scr[h] = _topk_select_t(_dot3_nt(kmp, qh), own, nblk)
            qhb = (qh * LOG2E).astype(BF16)
            qh_scr[h] = qhb
            st = jnp.where(causal, _dot_nt(kb, qhb), NEG)
            s_scr[h, own] = st
            ms.append(jnp.max(st, axis=0, keepdims=True))

    def pass_a(n, ms):
        rows = pl.ds(pl.multiple_of(n * MOBA_BLOCK, MOBA_BLOCK), MOBA_BLOCK)
        out = []
        for grp in range(ngrp):
            kb = k_ref[rows, grp * LANES:(grp + 1) * LANES].astype(BF16)
            for sub in range(HEADS_PER_VREG):
                h = grp * HEADS_PER_VREG + sub
                selrow = jnp.max(jnp.where(rown == n, sel_scr[h], 0.0), axis=0, keepdims=True)
                st = jnp.where(selrow > 0.0, _dot_nt(kb, qh_scr[h]), NEG)
                s_scr[h, n] = st
                out.append(jnp.maximum(ms[h], jnp.max(st, axis=0, keepdims=True)))
        return tuple(out)

    def blocks(step, first, count, carry, unroll):
        while unroll >= 1:
            def body(i, c, first=first, unroll=unroll):
                for u in range(unroll):
                    c = step(first + unroll * i + u, c)
                return c
            trips = count // unroll
            carry = lax.fori_loop(0, trips, body, carry)
            first, count, unroll = first + trips * unroll, count - trips * unroll, unroll // 2
        return carry

    ms = blocks(pass_a, 0, own, tuple(ms), 4)

    acc_scr[...] = jnp.zeros_like(acc_scr)
    ones = jnp.ones((acc_scr.shape[1] - AT_DH, MOBA_BLOCK), BF16)

    def pass_b(n, carry):
        cols = pl.ds(pl.multiple_of(n * MOBA_BLOCK, MOBA_BLOCK), MOBA_BLOCK)
        for h in range(AT_HEADS):
            p = jnp.exp2(s_scr[h, n] - ms[h]).astype(BF16)
            vte = jnp.concatenate([vt_ref[0, h * AT_DH:(h + 1) * AT_DH, cols].astype(BF16), ones], axis=0)
            acc_scr[h] = acc_scr[h] + jnp.dot(vte, p, preferred_element_type=F32)
        return carry

    blocks(pass_b, 0, own + 1, 0, 4)
    for grp in range(ngrp):
        parts = []
        for h in range(grp * HEADS_PER_VREG, (grp + 1) * HEADS_PER_VREG):
            acc = acc_scr[h]
            parts.append(acc[:AT_DH] / acc[AT_DH:AT_DH + 1])
        ot = jnp.concatenate(parts, axis=0)
        o_ref[:, grp * LANES:(grp + 1) * LANES] = ot.T.astype(o_ref.dtype)


def _attn_prompt(q, k, vt, *, bsz, t):
    n, w = q.shape
    nblk = t // MOBA_BLOCK
    assert t % MOBA_BLOCK == 0
    rpad = -(-nblk // SUBLANES) * SUBLANES
    return pl.pallas_call(
        functools.partial(_attn_kernel, nblk=nblk),
        grid=(bsz, nblk),
        in_specs=[
            pl.BlockSpec((MOBA_BLOCK, w), lambda b, i: (b * nblk + i, 0)),
            pl.BlockSpec((t, w), lambda b, i: (b, 0)),
            pl.BlockSpec((1, w, t), lambda b, i: (b, 0, 0)),
        ],
        out_specs=pl.BlockSpec((MOBA_BLOCK, w), lambda b, i: (b * nblk + i, 0)),
        out_shape=jax.ShapeDtypeStruct((n, w), BF16),
        scratch_shapes=[pltpu.VMEM((rpad, w), F32),
                        pltpu.VMEM((AT_HEADS, MOBA_BLOCK, LANES), BF16),
                        pltpu.VMEM((AT_HEADS, rpad, MOBA_BLOCK), F32),
                        pltpu.VMEM((AT_HEADS, nblk, MOBA_BLOCK, MOBA_BLOCK), F32),
                        pltpu.VMEM((AT_HEADS, AT_DH + 2 * SUBLANES, MOBA_BLOCK), F32)],
        compiler_params=_cparams(("arbitrary", "arbitrary")),
        name="moba_prompt",
    )(q, k, vt)


def _samp_scores_kernel(*refs, npg, pages_per_blk):
    _scores_body(refs[1:5], refs[5:5 + npg], refs[5 + npg:], pl.program_id(1), pl.num_programs(1),
                 pages_per_blk=pages_per_blk)


def _scores_body(q_refs, kpages, out_refs, g, ng, *, pages_per_blk):
    qbd_ref, qm_ref, kn_ref, vn_ref = q_refs
    p_ref, oown_ref, pk_ref, s_scr, ks_scr = out_refs
    npg = len(kpages)
    nh, dh, ps = kpages[0].shape[2:]
    rows = qm_ref.shape[1]
    scale = dh ** -0.5
    lane = lax.broadcasted_iota(jnp.int32, (1, LANES), 1)

    @pl.when(g == 0)
    def _zero():
        ks_scr[...] = jnp.zeros_like(ks_scr)

    qb = qbd_ref[0] * scale
    qh = qb.astype(BF16)
    for i in range(npg // pages_per_blk):
        pages = [kpages[i * pages_per_blk + u][0, 0].reshape(nh * dh, ps) for u in range(pages_per_blk)]
        n = g * (npg // pages_per_blk) + i
        tot = pages[0]
        for pg in pages[1:]:
            tot = tot + pg
        ks_scr[...] = jnp.where(lane == n, jnp.sum(tot, axis=-1, keepdims=True), ks_scr[...])
        kp = pages[0] if pages_per_blk == 1 else jnp.concatenate(pages, axis=1)
        s_scr[:, pl.ds(pl.multiple_of(n * pages_per_blk * ps, ps), pages_per_blk * ps)] = jnp.dot(
            qh, kp.astype(BF16), preferred_element_type=F32)

    @pl.when(g == ng - 1)
    def _finish():
        total = s_scr.shape[1]
        bw = pages_per_blk * ps
        nblk = total // bw
        gate = jnp.where(lane < nblk, _dot3(qb, ks_scr[...] * (1.0 / bw)), NEG)
        lanef = lane.astype(F32)
        cur = gate
        picks = []
        for _ in range(min(MOBA_TOPK, nblk)):
            mx = jnp.max(cur, axis=-1, keepdims=True)
            idx = jnp.min(jnp.where(cur == mx, lanef, float(LANES)), axis=-1, keepdims=True)
            picks.append(idx)
            cur = jnp.where(lanef == idx, NEG, cur)
        pk = jnp.zeros((rows, LANES), F32)
        for j, idx in enumerate(picks):
            pk = jnp.where(lane == j, idx, pk)
        pk_ref[0] = pk.astype(jnp.int32)
        qm = qm_ref[0] * scale
        so = _dot_nt(qm, kn_ref[0])
        ro = lax.broadcasted_iota(jnp.int32, (rows, 1), 0)
        co = lax.broadcasted_iota(jnp.int32, (1, rows), 1)
        own_ok = jnp.logical_and(ro % nh == co % nh, co // nh <= ro // nh)
        so = jnp.where(own_ok, so, NEG)
        m = jnp.max(so, axis=-1, keepdims=True)
        lchunk = 8 * bw if total % (8 * bw) == 0 else bw
        nch = total // lchunk
        for cidx in range(nch):
            cs = slice(cidx * lchunk, (cidx + 1) * lchunk)
            bid = (cidx * (lchunk // bw) + lax.broadcasted_iota(jnp.int32, (1, lchunk), 1) // bw).astype(F32)
            hit = jnp.zeros((rows, lchunk), F32)
            for idx in picks:
                hit = jnp.where(bid == idx, 1.0, hit)
            sc = jnp.where(hit > 0.0, s_scr[:, cs], NEG)
            s_scr[:, cs] = sc
            m = jnp.maximum(m, jnp.max(sc, axis=-1, keepdims=True))
        po = jnp.exp(so - m)
        l = jnp.sum(po, axis=-1, keepdims=True)
        for cidx in range(nch):
            cs = slice(cidx * lchunk, (cidx + 1) * lchunk)
            pc = jnp.exp(s_scr[:, cs] - m)
            s_scr[:, cs] = pc
            l = l + jnp.sum(pc, axis=-1, keepdims=True)
        inv = 1.0 / l
        for cidx in range(nch):
            cs = slice(cidx * lchunk, (cidx + 1) * lchunk)
            p_ref[0, :, cs] = s_scr[:, cs] * inv
        oown_ref[0] = _dot(po * inv, vn_ref[0])


def _samp_pv_kernel(pt_ref, pk_ref, p_ref, oown_ref, cv_hbm, o_ref, vbuf, sem, *, layer, topk, pages_per_blk):
    b = pl.program_id(0)
    rows = p_ref.shape[1]
    nh, dh, ps = cv_hbm.shape[2:]

    def slice_copy(r, j, pg):
        page = pt_ref[b, pk_ref[b, r * topk + j] * pages_per_blk + pg]
        return pltpu.make_async_copy(cv_hbm.at[layer, page, r % nh], vbuf.at[r, j * pages_per_blk + pg], sem.at[r])

    slots = [(j, pg) for j in range(topk) for pg in range(pages_per_blk)]
    for r in range(rows):
        for j, pg in slots:
            slice_copy(r, j, pg).start()

    lane = lax.broadcasted_iota(jnp.int32, (1, LANES), 1)
    ot = jnp.zeros((dh, LANES), F32)
    for r in range(rows):
        for j, pg in slots:
            pltpu.make_async_copy(cv_hbm.at[layer, 0, 0], vbuf.at[r, j * pages_per_blk + pg], sem.at[r]).wait()
        acc = jnp.zeros((dh, ps), F32)
        for j, pg in slots:
            off = pl.multiple_of((pk_ref[b, r * topk + j] * pages_per_blk + pg) * ps, ps)
            acc = acc + vbuf[r, j * pages_per_blk + pg] * p_ref[0, r:r + 1, pl.ds(off, ps)]
        ot = jnp.where(lane == r, jnp.sum(acc, axis=-1, keepdims=True), ot)
    o_ref[0] = oown_ref[0] + ot.T[:rows, :]


def _scores_plan(qm, kn, vn, cache_k, page_table, *, layer, seq_g):
    bs, rows, dh = qm.shape
    _, _, ps, nh, _ = cache_k.shape
    npages = page_table.shape[1]
    pages_per_blk = MOBA_BLOCK // ps
    assert MOBA_BLOCK % ps == 0 and npages % pages_per_blk == 0 and npages // pages_per_blk <= LANES
    assert rows <= LANES
    npg = math.gcd(npages, PAGES_PER_STEP)
    assert npg % pages_per_blk == 0
    total = npages * ps
    ck = jnp.transpose(cache_k, (0, 1, 3, 4, 2))
    head_of_row = jnp.arange(rows, dtype=jnp.int32) % nh
    head_of_col = jnp.arange(nh * dh, dtype=jnp.int32) // dh
    qbd = jnp.where(head_of_row[:, None] == head_of_col[None, :], jnp.tile(qm, (1, 1, nh)), 0.0)

    def per_seq(width):
        return pl.BlockSpec((1, rows, width), lambda *idx: (seq_g(*idx[:-1])[0], 0, 0))

    def page_spec(j):
        def index_map(*idx):
            seq, g = seq_g(*idx[:-1])
            return (layer, idx[-1][seq, g * npg + j], 0, 0, 0)
        return pl.BlockSpec((1, 1, nh, dh, ps), index_map)

    return dict(
        npg=npg, pages_per_blk=pages_per_blk, groups=npages // npg, bs=bs,
        in_specs=[per_seq(nh * dh), per_seq(dh), per_seq(dh), per_seq(dh)] + [page_spec(j) for j in range(npg)],
        args=[qbd, qm, kn, vn] + [ck] * npg,
        out_specs=[per_seq(total), per_seq(dh), per_seq(LANES)],
        out_shape=[jax.ShapeDtypeStruct((bs, rows, total), F32), jax.ShapeDtypeStruct((bs, rows, dh), F32),
                   jax.ShapeDtypeStruct((bs, rows, LANES), jnp.int32)],
        scratch=[pltpu.VMEM((rows, total), F32), pltpu.VMEM((nh * dh, LANES), F32)],
    )


def _samp_scores(plan, page_table):
    return pl.pallas_call(
        functools.partial(_samp_scores_kernel, npg=plan["npg"], pages_per_blk=plan["pages_per_blk"]),
        grid_spec=pltpu.PrefetchScalarGridSpec(
            num_scalar_prefetch=1, grid=(plan["bs"], plan["groups"]),
            in_specs=plan["in_specs"], out_specs=plan["out_specs"], scratch_shapes=plan["scratch"]),
        out_shape=plan["out_shape"],
        compiler_params=_cparams(("arbitrary", "arbitrary")),
        name="moba_sample_scores",
    )(page_table, *plan["args"])


def _hgrn_with_scores(hplan, splan, page_table):
    outs = pl.pallas_call(
        functools.partial(_hgrn_scores_kernel, n_hg_in=len(hplan["in_specs"]), npg=splan["npg"],
                          hgrn_cfg=hplan["cfg"], pages_per_blk=splan["pages_per_blk"]),
        grid_spec=pltpu.PrefetchScalarGridSpec(
            num_scalar_prefetch=1, grid=hplan["grid"],
            in_specs=hplan["in_specs"] + splan["in_specs"],
            out_specs=hplan["out_specs"] + splan["out_specs"],
            scratch_shapes=hplan["scratch"] + splan["scratch"]),
        out_shape=hplan["out_shape"] + splan["out_shape"],
        compiler_params=_cparams(("arbitrary", "arbitrary", "arbitrary")),
        name="hgrn2_with_sample_scores",
    )(page_table, *hplan["args"], *splan["args"])
    return outs[:2], outs[2:]


def _samp_pv(probs, o_own, picks, cache_v, page_table, *, layer):
    bs, rows, total = probs.shape
    _, _, ps, nh, dh = cache_v.shape
    pages_per_blk = MOBA_BLOCK // ps
    topk = min(MOBA_TOPK, page_table.shape[1] // pages_per_blk)
    cv = jnp.transpose(cache_v, (0, 1, 3, 4, 2))
    picks = picks[:, :, :topk].reshape(bs, rows * topk)
    seq_spec = lambda w: pl.BlockSpec((1, rows, w), lambda b, pt, pk: (b, 0, 0))
    return pl.pallas_call(
        functools.partial(_samp_pv_kernel, layer=layer, topk=topk, pages_per_blk=pages_per_blk),
        grid_spec=pltpu.PrefetchScalarGridSpec(
            num_scalar_prefetch=2,
            grid=(bs,),
            in_specs=[seq_spec(total), seq_spec(dh), pl.BlockSpec(memory_space=pl.ANY)],
            out_specs=seq_spec(dh),
            scratch_shapes=[pltpu.VMEM((rows, topk * pages_per_blk, dh, ps), F32), pltpu.SemaphoreType.DMA((rows,))],
        ),
        out_shape=jax.ShapeDtypeStruct((bs, rows, dh), F32),
        compiler_params=_cparams(("arbitrary",)),
        name="moba_sample_pv",
    )(page_table, picks, probs, o_own, cv)


def _mix_ffn_kernel(x_ref, oa_ref, ob_ref, gate_ref, gt1_ref, sc2_ref, sh2_ref, gt2_ref,
                    gpost1_ref, gpre2_ref, gpost2_ref, wa_ref, wb_ref, wo_ref, wgu_ref, wd_ref, out_ref,
                    *, dff, fchunk):
    d = x_ref.shape[1]
    merged = gate_ref[:, :d] * _dot(oa_ref[...], wa_ref[...]) + gate_ref[:, d:] * _dot(ob_ref[...], wb_ref[...])
    x1 = x_ref[...] + gt1_ref[0] * _rms(_dot(merged, wo_ref[...]), gpost1_ref[...])
    hb = (_rms(x1, gpre2_ref[...]) * (1.0 + sc2_ref[0]) + sh2_ref[0]).astype(BF16)
    acc = jnp.zeros(x1.shape, F32)
    for c0 in range(0, dff, fchunk):
        c1 = min(c0 + fchunk, dff)
        g = _dot(hb, wgu_ref[:, c0:c1])
        u = _dot(hb, wgu_ref[:, dff + c0:dff + c1])
        acc = acc + _dot(g * _sigmoid(g) * u, wd_ref[c0:c1, :])
    out_ref[...] = x1 + gt2_ref[0] * _rms(acc, gpost2_ref[...])


def _mix_ffn(x, oa, ob, gates, gt1, sc2, sh2, gt2, g_post1, g_pre2, g_post2, wa, wb, wo, wgu, wd, tm):
    n, d = x.shape
    dff = wd.shape[0]
    nt = n // tm
    nbm, r, _ = gt1.shape
    tps = nt // nbm
    fchunk = 3 * MXU_WIDTH
    resident = dict(pipeline_mode=pl.Buffered(1))
    rowspec = lambda w: pl.BlockSpec((tm, w), lambda i: (i, 0))
    modspec = pl.BlockSpec((1, r, d), lambda i: (i // tps, 0, 0))
    vecspec = pl.BlockSpec((1, d), lambda i: (0, 0))
    wspec = lambda w: pl.BlockSpec(w.shape, lambda i: (0, 0), **resident)
    return pl.pallas_call(
        functools.partial(_mix_ffn_kernel, dff=dff, fchunk=fchunk),
        grid=(nt,),
        in_specs=[rowspec(d), rowspec(oa.shape[1]), rowspec(ob.shape[1]), rowspec(gates.shape[1]),
                  modspec, modspec, modspec, modspec, vecspec, vecspec, vecspec,
                  wspec(wa), wspec(wb), wspec(wo), wspec(wgu), wspec(wd)],
        out_specs=rowspec(d),
        out_shape=jax.ShapeDtypeStruct((n, d), F32),
        compiler_params=_cparams(("arbitrary",)),
        name="merge_proj_ffn",
    )(x, oa, ob, gates, gt1, sc2, sh2, gt2, g_post1.reshape(1, d), g_pre2.reshape(1, d), g_post2.reshape(1, d),
      wa, wb, wo, wgu, wd)


def _rope_tables(pos):
    half = AT_DH // 2
    inv = ROPE_THETA ** (-jnp.arange(half, dtype=F32) / half)
    ang = pos.astype(F32)[:, None] * inv[None, :]
    cos = jnp.cos(ang)
    sin = jnp.sin(ang)
    cos_h = jnp.concatenate([cos, cos], axis=-1)
    sin_h = jnp.concatenate([-sin, sin], axis=-1)
    return jnp.tile(cos_h, (1, HEADS_PER_VREG)), jnp.tile(sin_h, (1, HEADS_PER_VREG))


def _project(x2d, mods, *, t, pos0, weights, tm, transposed_kv):
    sh1, sc1 = mods[0], mods[1]
    g_pre1, w_in_hi, wqk_lo = weights[1], weights[3], weights[4]
    cos_t, sin_t = _rope_tables(pos0 + jnp.arange(t, dtype=jnp.int32))
    if tm > t:
        cos_t = jnp.tile(cos_t, (tm // t, 1))
        sin_t = jnp.tile(sin_t, (tm // t, 1))
    return _in_proj(x2d, sc1, sh1, g_pre1, cos_t, sin_t, w_in_hi, wqk_lo, tm, seq_len=t if transposed_kv else None)


def _layer(xp, xs, mods_p, mods_s, s0_s, cache_k, cache_v, page_table, *, layer, weights, dims):
    bp, t, bs, ts, past_len = dims
    (hg_lb, g_pre1, g_post1, w_in_hi, wqk_lo, g_onorm, wa, wb, wo, g_pre2, g_post2, wgu, wd) = weights
    tm_p, tm_s = 512, bs * ts
    hg4_p, q_p, k_p, kt_p, vt_p, gates_p = _project(xp, mods_p, t=t, pos0=0, weights=weights, tm=tm_p,
                                                     transposed_kv=True)
    hg4_s, q_s, k_s, v_s, gates_s = _project(xs, mods_s, t=ts, pos0=past_len, weights=weights, tm=tm_s,
                                             transposed_kv=False)

    chunk = HG_CHUNK if t % HG_CHUNK == 0 else t
    hplan = _hgrn_plan(hg4_p, hg_lb, g_onorm, None, layer=layer, bsz=bp, t=t, c=chunk, bd=SUBLANES, valid=chunk,
                       rows_per_step=512, hps=1, out_dtype=BF16)
    to_rows = lambda a: a.reshape(bs, ts * AT_HEADS, AT_DH)
    groups = page_table.shape[1] // math.gcd(page_table.shape[1], PAGES_PER_STEP)
    _, hgroups, steps = hplan["grid"]
    fused = bp * hgroups * steps == bs * groups
    seq_g = ((lambda b, hg, cb: divmod((b * hgroups + hg) * steps + cb, groups)) if fused
             else (lambda b, g: (b, g)))
    splan = _scores_plan(to_rows(q_s), to_rows(k_s), to_rows(v_s), cache_k, page_table, layer=layer, seq_g=seq_g)
    if fused:
        (oa_p, state_p), (probs, o_own, picks) = _hgrn_with_scores(hplan, splan, page_table)
    else:
        oa_p, state_p = _hgrn(hg4_p, hg_lb, g_onorm, None, layer=layer, bsz=bp, t=t, c=chunk, bd=SUBLANES,
                              valid=chunk, rows_per_step=512, hps=1, out_dtype=BF16)
        probs, o_own, picks = _samp_scores(splan, page_table)
    ob_s = _samp_pv(probs, o_own, picks, cache_v, page_table, layer=layer).reshape(bs * ts, AT_HEADS * AT_DH)

    cs = -(-ts // SUBLANES) * SUBLANES
    hg4_sp = jnp.pad(hg4_s.reshape(bs, ts, -1), ((0, 0), (0, cs - ts), (0, 0))).reshape(bs * cs, -1)
    oa_s, state_s = _hgrn(hg4_sp, hg_lb, g_onorm, s0_s, layer=layer, bsz=bs, t=cs, c=cs, bd=cs, valid=ts,
                          rows_per_step=cs, hps=HG_HEADS, out_dtype=F32)
    oa_s = oa_s.reshape(bs, cs, -1)[:, :ts].reshape(bs * ts, -1)

    ob_p = _attn_prompt(q_p, k_p, vt_p, bsz=bp, t=t)
    from_t = lambda a: jnp.transpose(a.reshape(bp, AT_HEADS, AT_DH, t), (0, 3, 1, 2))
    tail = (g_post1, g_pre2, g_post2, wa, wb, wo, wgu, wd)
    xp = _mix_ffn(xp, oa_p, ob_p, gates_p, mods_p[2], mods_p[4], mods_p[3], mods_p[5], *tail, tm_p)
    xs = _mix_ffn(xs, oa_s, ob_s, gates_s, mods_s[2], mods_s[4], mods_s[3], mods_s[5], *tail, tm_s)
    return (xp, xs, state_p, state_s, from_t(kt_p), from_t(vt_p),
            k_s.reshape(bs, ts, AT_HEADS, AT_DH), v_s.reshape(bs, ts, AT_HEADS, AT_DH))


def kernel(x_prompt, x_sample, c_prompt, c_sample, state_hgrn, cache_k, cache_v, page_table, hg_lb, w_ada, b_ada,
           g_pre1, g_post1, w_in, g_onorm, w_proj_a, w_proj_b, w_out, g_pre2, g_post2, w_gu, w_down):
    bp, t, d = x_prompt.shape
    bs, ts, _ = x_sample.shape
    depth = w_in.shape[0]
    past_len = page_table.shape[1] * cache_k.shape[2]
    assert past_len % MOBA_BLOCK == 0 and ts <= MOBA_BLOCK
    hgw = 2 * HG_HEADS * HG_DK + 2 * HG_HEADS * HG_DV
    atw = AT_HEADS * AT_DH

    xp = x_prompt.reshape(bp * t, d)
    xs = x_sample.reshape(bs * ts, d)
    c_all = jnp.concatenate([c_prompt, c_sample], axis=0)
    outs = [[] for _ in range(6)]
    for l in range(depth):
        mod = _ada_mod(c_all, w_ada[l], b_ada[l])
        mods_p = [m[:bp].reshape(bp, 1, d) for m in jnp.split(mod, 6, axis=-1)]
        mods_s = [jnp.repeat(m[bp:], ts, axis=0).reshape(1, bs * ts, d) for m in jnp.split(mod, 6, axis=-1)]
        _, wqk_lo = _split_weights(w_in[l][:, hgw:hgw + 2 * atw])
        weights = (hg_lb, g_pre1[l], g_post1[l], w_in[l].astype(BF16), wqk_lo, g_onorm[l],
                   w_proj_a[l].astype(BF16), w_proj_b[l].astype(BF16), w_out[l].astype(BF16),
                   g_pre2[l], g_post2[l], w_gu[l].astype(BF16), w_down[l].astype(BF16))
        xp, xs, *rest = _layer(xp, xs, mods_p, mods_s, state_hgrn[l], cache_k, cache_v, page_table, layer=l,
                               weights=weights, dims=(bp, t, bs, ts, past_len))
        for lst, val in zip(outs, rest):
            lst.append(val)
    return (xp.reshape(bp, t, d), xs.reshape(bs, ts, d)) + tuple(jnp.stack(o) for o in outs)
```

```python
import functools
import math

import jax
import jax.numpy as jnp
from jax import lax
from jax.experimental import pallas as pl
from jax.experimental.pallas import tpu as pltpu

F32 = jnp.float32
BF16 = jnp.bfloat16

HG_HEADS = 4
HG_DK = 128
HG_DV = 128
HG_CHUNK = 128
AT_HEADS = 8
AT_DH = 64
MOBA_BLOCK = 256
MOBA_TOPK = 3
ROPE_THETA = 10000.0
EPS = 1e-6
NEG = -1e30
LOG2E = math.log2(math.e)

V7X_VMEM_LIMIT_BYTES = 52 * 1024 * 1024
LANES = 128
SUBLANES = 8
HEADS_PER_VREG = LANES // AT_DH
MXU_WIDTH = 256
PAGES_PER_STEP = 16


def _cparams(sem):
    return pltpu.CompilerParams(dimension_semantics=sem, vmem_limit_bytes=V7X_VMEM_LIMIT_BYTES)


def _dot(a, b):
    return jnp.dot(a.astype(BF16), b.astype(BF16), preferred_element_type=F32)


def _dot_nt(a, b):
    return lax.dot_general(a.astype(BF16), b.astype(BF16), (((1,), (1,)), ((), ())), preferred_element_type=F32)


def _dot_tn(a, b):
    return lax.dot_general(a.astype(BF16), b.astype(BF16), (((0,), (0,)), ((), ())), preferred_element_type=F32)


def _split2(a):
    hi = a.astype(BF16)
    lo = (a - hi.astype(F32)).astype(BF16)
    return hi, lo


def _split3(a):
    p1 = a.astype(BF16)
    r1 = a - p1.astype(F32)
    p2 = r1.astype(BF16)
    p3 = (r1 - p2.astype(F32)).astype(BF16)
    return p1, p2, p3


def _dot3(a, b):
    ah, al = _split2(a)
    bh, bl = _split2(b)
    return _dot(ah, bh) + _dot(ah, bl) + _dot(al, bh)


def _dot3_nt(a, b):
    ah, al = _split2(a)
    bh, bl = _split2(b)
    return _dot_nt(ah, bh) + _dot_nt(ah, bl) + _dot_nt(al, bh)


def _sigmoid(x):
    return 1.0 / (1.0 + jnp.exp(-x))


def _rms(x, w):
    return x * lax.rsqrt(jnp.mean(x * x, axis=-1, keepdims=True) + EPS) * w


def _ada_kernel(c_ref, w_ref, b_ref, o_ref):
    c = c_ref[...]
    o_ref[...] = _dot3(c * _sigmoid(c), w_ref[...]) + b_ref[...]


def _ada_mod(c_all, w_ada, b_ada):
    rows, d = c_all.shape
    n = w_ada.shape[1]
    tn = 1024
    return pl.pallas_call(
        _ada_kernel,
        grid=(n // tn,),
        in_specs=[
            pl.BlockSpec((rows, d), lambda j: (0, 0)),
            pl.BlockSpec((d, tn), lambda j: (0, j)),
            pl.BlockSpec((1, tn), lambda j: (0, j)),
        ],
        out_specs=pl.BlockSpec((rows, tn), lambda j: (0, j)),
        out_shape=jax.ShapeDtypeStruct((rows, n), F32),
        compiler_params=_cparams(("arbitrary",)),
        name="ada_mod",
    )(c_all, w_ada, b_ada.reshape(1, n))


def _split_kernel(w_ref, hi_ref, lo_ref):
    hi, lo = _split2(w_ref[...])
    hi_ref[...] = hi
    lo_ref[...] = lo


def _split_weights(w):
    r, c = w.shape
    tr = 256
    return pl.pallas_call(
        _split_kernel,
        grid=(r // tr,),
        in_specs=[pl.BlockSpec((tr, c), lambda i: (i, 0))],
        out_specs=[pl.BlockSpec((tr, c), lambda i: (i, 0))] * 2,
        out_shape=[jax.ShapeDtypeStruct((r, c), BF16)] * 2,
        compiler_params=_cparams(("arbitrary",)),
        name="split_weights",
    )(w)


def _rope_group(y, cos, sin_signed):
    lane = lax.broadcasted_iota(jnp.int32, (1, LANES), 1)
    first_half = (lane % AT_DH) < (AT_DH // 2)
    rot = jnp.where(first_half, pltpu.roll(y, LANES - AT_DH // 2, 1), pltpu.roll(y, AT_DH // 2, 1))
    return y * cos + rot * sin_signed


def _in_proj_kernel(x_ref, sc_ref, sh_ref, g_ref, cos_ref, sin_ref, w_ref, wlo_ref, *outs, hgw, atw, kv_t):
    if kv_t:
        hg_ref, q_ref, k_ref, kt_ref, vt_ref, gate_ref = outs
    else:
        hg_ref, q_ref, k_ref, v_ref, gate_ref = outs
    x = x_ref[...]
    h = _rms(x, g_ref[...]) * (1.0 + sc_ref[0]) + sh_ref[0]
    hh, hl = _split2(h)
    seg = 512
    for j in range(hgw // seg):
        hg_ref[:, j * seg:(j + 1) * seg] = _dot(hh, w_ref[:, j * seg:(j + 1) * seg])
    tm = x.shape[0]
    hs = jnp.concatenate([hh, hl], axis=0)
    for idx, out in enumerate((q_ref, k_ref)):
        c0 = hgw + idx * atw
        a = jnp.dot(hs, w_ref[:, c0:c0 + atw], preferred_element_type=F32)
        y = a[:tm] + a[tm:] + _dot(hh, wlo_ref[:, idx * atw:(idx + 1) * atw])
        for g in range(atw // LANES):
            yg = _rope_group(y[:, g * LANES:(g + 1) * LANES], cos_ref[...], sin_ref[...])
            out[:, g * LANES:(g + 1) * LANES] = yg
            if kv_t and idx == 1:
                kt_ref[0, g * LANES:(g + 1) * LANES, :] = yg.T
    c0 = hgw + 2 * atw
    v = _dot(hh, w_ref[:, c0:c0 + atw])
    if kv_t:
        for g in range(atw // LANES):
            vt_ref[0, g * LANES:(g + 1) * LANES, :] = v[:, g * LANES:(g + 1) * LANES].T
    else:
        v_ref[...] = v
    c0 = hgw + 3 * atw
    gw = gate_ref.shape[1]
    for j in range(gw // seg):
        gate = _sigmoid(_dot(hh, w_ref[:, c0 + j * seg:c0 + (j + 1) * seg]))
        gate_ref[:, j * seg:(j + 1) * seg] = gate.astype(gate_ref.dtype)


def _in_proj(x, sc, sh, g_pre, cos_t, sin_t, w_hi, wqk_lo, tm, seq_len=None):
    n, d = x.shape
    kv_t = seq_len is not None
    hgw = 2 * HG_HEADS * HG_DK + 2 * HG_HEADS * HG_DV
    atw = AT_HEADS * AT_DH
    gw = 2 * d
    ncols = w_hi.shape[1]
    assert ncols == hgw + 3 * atw + gw
    nt = n // tm
    nbm, r, _ = sc.shape
    tps = nt // nbm
    ctiles = cos_t.shape[0] // tm
    resident = dict(pipeline_mode=pl.Buffered(1))
    rowspec = lambda w: pl.BlockSpec((tm, w), lambda i: (i, 0))
    rowshape = lambda w: jax.ShapeDtypeStruct((n, w), F32)
    if kv_t:
        tiles_per_seq = seq_len // tm
        tspec = pl.BlockSpec((1, atw, tm), lambda i: (i // tiles_per_seq, 0, i % tiles_per_seq))
        tshape = jax.ShapeDtypeStruct((n // seq_len, atw, seq_len), F32)
        kv_specs, kv_shapes = [rowspec(atw), tspec, tspec], [rowshape(atw), tshape, tshape]
    else:
        kv_specs, kv_shapes = [rowspec(atw), rowspec(atw)], [rowshape(atw), rowshape(atw)]
    return pl.pallas_call(
        functools.partial(_in_proj_kernel, hgw=hgw, atw=atw, kv_t=kv_t),
        grid=(nt,),
        in_specs=[
            pl.BlockSpec((tm, d), lambda i: (i, 0)),
            pl.BlockSpec((1, r, d), lambda i: (i // tps, 0, 0)),
            pl.BlockSpec((1, r, d), lambda i: (i // tps, 0, 0)),
            pl.BlockSpec((1, d), lambda i: (0, 0)),
            pl.BlockSpec((tm, LANES), lambda i: (i % ctiles, 0)),
            pl.BlockSpec((tm, LANES), lambda i: (i % ctiles, 0)),
            pl.BlockSpec((d, ncols), lambda i: (0, 0), **resident),
            pl.BlockSpec((d, 2 * atw), lambda i: (0, 0), **resident),
        ],
        out_specs=[rowspec(hgw), rowspec(atw)] + kv_specs + [rowspec(gw)],
        out_shape=[rowshape(hgw), rowshape(atw)] + kv_shapes + [jax.ShapeDtypeStruct((n, gw), BF16)],
        compiler_params=_cparams(("arbitrary",)),
        name="in_proj",
    )(x, sc, sh, g_pre.reshape(1, d), cos_t, sin_t, w_hi, wqk_lo)


def _cumsum_rows(g):
    c = g.shape[0]
    row = lax.broadcasted_iota(jnp.int32, (c, c), 0)
    col = lax.broadcasted_iota(jnp.int32, (c, c), 1)
    tri = jnp.where(row >= col, 1.0, 0.0).astype(BF16)
    p1, p2, p3 = _split3(g)
    return (jnp.dot(tri, p1, preferred_element_type=F32) + jnp.dot(tri, p2, preferred_element_type=F32)
            + jnp.dot(tri, p3, preferred_element_type=F32))


def _hgrn_chunk(q, hf, v, hg, lb, gon, st, *, c, bd, valid):
    f = lb + (1.0 - lb) * _sigmoid(hf)
    gl = jnp.log2(f)
    kin = 1.0 - f
    row = lax.broadcasted_iota(jnp.int32, (c, 1), 0)
    if valid < c:
        live = row < valid
        gl = jnp.where(live, gl, 0.0)
        kin = jnp.where(live, kin, 0.0)
    b = _cumsum_rows(gl)
    o = _dot_nt(q * jnp.exp2(b), st)
    a = None
    m = c // 2
    while m >= bd:
        span = 2 * m
        right = (row % span) >= m
        npar = c // span
        ref = b[m - 1:m]
        for p in range(1, npar):
            ref = jnp.where(row // span == p, b[p * span + m - 1:p * span + m], ref)
        eq = jnp.exp2(jnp.where(right, b - ref, NEG))
        ek = jnp.exp2(jnp.where(right, NEG, ref - b))
        al = _dot_nt(q * eq, kin * ek)
        if npar > 1:
            rp = lax.broadcasted_iota(jnp.int32, (c, c), 0) // span
            cp = lax.broadcasted_iota(jnp.int32, (c, c), 1) // span
            al = jnp.where(rp == cp, al, 0.0)
        a = al if a is None else a + al
        m //= 2
    if a is not None:
        o = o + _dot(a, v)
    trow = lax.broadcasted_iota(jnp.int32, (bd, 1), 0)
    blocks = []
    for i in range(c // bd):
        sl = slice(i * bd, (i + 1) * bd)
        bi, qi, ki, vi = b[sl], q[sl], kin[sl], v[sl]
        od = jnp.zeros((bd, v.shape[1]), F32)
        for s in range(bd):
            d = bi - bi[s:s + 1]
            e = jnp.exp2(d if s == 0 else jnp.where(trow >= s, d, NEG))
            ac = jnp.sum(qi * e * ki[s:s + 1], axis=-1, keepdims=True)
            od = od + ac * vi[s:s + 1]
        blocks.append(od)
    o = o + (blocks[0] if len(blocks) == 1 else jnp.concatenate(blocks, axis=0))
    bl = b[c - 1:c]
    st_new = st * jnp.exp2(bl) + _dot_tn(v, kin * jnp.exp2(bl - b))
    y = _rms(o, gon) * (hg * _sigmoid(hg))
    return y, st_new


def _hgrn_kernel(*refs, **cfg):
    _hgrn_body(refs, pl.program_id(2), pl.num_programs(2), **cfg)


def _hgrn_body(refs, cb, ncb, *, layer, c, bd, valid, n_inner, use_s0, hps):
    if use_s0:
        lb_ref, gon_ref, q_ref, f_ref, i_ref, g_ref, s0_ref, o_ref, sout_ref, st_scr = refs
    else:
        lb_ref, gon_ref, q_ref, f_ref, i_ref, g_ref, o_ref, sout_ref, st_scr = refs

    @pl.when(cb == 0)
    def _init():
        for hh in range(hps):
            st_scr[hh] = s0_ref[0, hh].T if use_s0 else jnp.zeros(st_scr.shape[1:], F32)

    rows = [lb_ref[j:j + 1, :] for j in range(lb_ref.shape[0])]
    mx = rows[0]
    for r in rows[1:]:
        mx = jnp.maximum(mx, r)
    es = [jnp.exp(r - mx) for r in rows]
    tot = es[0]
    for e in es[1:]:
        tot = tot + e
    part = es[0]
    for e in es[1:layer + 1]:
        part = part + e
    lb_all = part / tot
    gon = gon_ref[...]

    for hh in range(hps):
        ln = slice(hh * HG_DK, (hh + 1) * HG_DK)
        lb = lb_all[:, ln]
        st = st_scr[hh]
        for j in range(n_inner):
            sl = slice(j * c, (j + 1) * c)
            y, st = _hgrn_chunk(q_ref[sl, ln], f_ref[sl, ln], i_ref[sl, ln], g_ref[sl, ln], lb, gon, st,
                                c=c, bd=bd, valid=valid)
            o_ref[sl, ln] = y.astype(o_ref.dtype)
        st_scr[hh] = st

    @pl.when(cb == ncb - 1)
    def _fin():
        for hh in range(hps):
            sout_ref[0, hh] = st_scr[hh].T


def _hgrn_scores_kernel(*refs, n_hg_in, npg, hgrn_cfg, pages_per_blk):
    ins = refs[1:]
    hg_in, sc_q, kpages = ins[:n_hg_in], ins[n_hg_in:n_hg_in + 4], ins[n_hg_in + 4:n_hg_in + 4 + npg]
    rest = ins[n_hg_in + 4 + npg:]
    hg_out, sc_out, st_scr, s_scr, ks_scr = rest[:2], rest[2:5], rest[5], rest[6], rest[7]
    _hgrn_body(tuple(hg_in) + tuple(hg_out) + (st_scr,), pl.program_id(2), pl.num_programs(2), **hgrn_cfg)
    lin = (pl.program_id(0) * pl.num_programs(1) + pl.program_id(1)) * pl.num_programs(2) + pl.program_id(2)
    ng = s_scr.shape[1] // (npg * kpages[0].shape[4])
    _scores_body(sc_q, kpages, tuple(sc_out) + (s_scr, ks_scr), lin % ng, ng, pages_per_blk=pages_per_blk)


def _hgrn(hg4, hg_lb, g_onorm, s0, *, layer, bsz, t, c, bd, valid, rows_per_step, hps, out_dtype):
    plan = _hgrn_plan(hg4, hg_lb, g_onorm, s0, layer=layer, bsz=bsz, t=t, c=c, bd=bd, valid=valid,
                      rows_per_step=rows_per_step, hps=hps, out_dtype=out_dtype)
    return pl.pallas_call(
        functools.partial(_hgrn_kernel, **plan["cfg"]),
        grid=plan["grid"],
        in_specs=plan["in_specs"],
        out_specs=plan["out_specs"],
        out_shape=plan["out_shape"],
        scratch_shapes=plan["scratch"],
        compiler_params=_cparams(("arbitrary", "arbitrary", "arbitrary")),
        name="hgrn2",
    )(*plan["args"])


def _hgrn_plan(hg4, hg_lb, g_onorm, s0, *, layer, bsz, t, c, bd, valid, rows_per_step, hps, out_dtype):
    n = hg4.shape[0]
    h = HG_HEADS
    hgroups = h // hps
    steps = t // rows_per_step
    use_s0 = s0 is not None

    def col(k):
        return pl.BlockSpec((rows_per_step, hps * HG_DK),
                            lambda b, hg, cb, *_, k=k: (b * steps + cb, k * hgroups + hg))

    state_spec = pl.BlockSpec((1, hps, HG_DK, HG_DV), lambda b, hg, cb, *_: (b, hg, 0, 0))
    in_specs = [
        pl.BlockSpec((hg_lb.shape[0], hps * HG_DK), lambda b, hg, cb, *_: (0, hg)),
        pl.BlockSpec((1, HG_DV), lambda b, hg, cb, *_: (0, 0)),
        col(0), col(1), col(2), col(3),
    ]
    args = [hg_lb, g_onorm.reshape(1, HG_DV), hg4, hg4, hg4, hg4]
    if use_s0:
        in_specs.append(state_spec)
        args.append(s0)
    return dict(
        cfg=dict(layer=layer, c=c, bd=bd, valid=valid, n_inner=rows_per_step // c, use_s0=use_s0, hps=hps),
        grid=(bsz, hgroups, steps), in_specs=in_specs, args=args,
        out_specs=[pl.BlockSpec((rows_per_step, hps * HG_DV), lambda b, hg, cb, *_: (b * steps + cb, hg)),
                   state_spec],
        out_shape=[jax.ShapeDtypeStruct((n, h * HG_DV), out_dtype),
                   jax.ShapeDtypeStruct((bsz, h, HG_DK, HG_DV), F32)],
        scratch=[pltpu.VMEM((hps, HG_DV, HG_DK), F32)],
    )


def _topk_select_t(g, own, nblk):
    rown = lax.broadcasted_iota(jnp.int32, (g.shape[0], 1), 0)
    past = rown < own
    gm = jnp.where(past, g, NEG)
    rank = jnp.zeros(g.shape, F32)
    for m in range(nblk - 1):
        grow = gm[m:m + 1, :]
        tie = jnp.where(rown > m, 1.0, 0.0)
        rank = rank + jnp.where(grow > gm, 1.0, 0.0) + jnp.where(grow == gm, tie, 0.0)
    return jnp.where(past, jnp.where(rank < MOBA_TOPK, 1.0, 0.0), 0.0)


def _attn_kernel(q_ref, k_ref, vt_ref, o_ref, km_scr, qh_scr, sel_scr, s_scr, acc_scr, *, nblk):
    tq = MOBA_BLOCK
    own = pl.program_id(1)
    scale = AT_DH ** -0.5
    ngrp = AT_HEADS // HEADS_PER_VREG

    @pl.when(own == 0)
    def _means():
        km_scr[...] = jnp.zeros_like(km_scr)
        for n in range(nblk):
            blk = k_ref[n * MOBA_BLOCK:(n + 1) * MOBA_BLOCK, :]
            km_scr[n:n + 1, :] = jnp.sum(blk, axis=0, keepdims=True) * (1.0 / MOBA_BLOCK)

    lane = lax.broadcasted_iota(jnp.int32, (1, LANES), 1)
    rown = lax.broadcasted_iota(jnp.int32, (km_scr.shape[0], 1), 0)
    causal = (lax.broadcasted_iota(jnp.int32, (MOBA_BLOCK, tq), 0)
              <= lax.broadcasted_iota(jnp.int32, (MOBA_BLOCK, tq), 1))
    r_own = pl.multiple_of(own * MOBA_BLOCK, MOBA_BLOCK)

    ms = []
    for grp in range(ngrp):
        lo = grp * LANES
        qp = q_ref[:, lo:lo + LANES]
        kmp = km_scr[:, lo:lo + LANES]
        kb = k_ref[pl.ds(r_own, MOBA_BLOCK), lo:lo + LANES].astype(BF16)
        for sub in range(HEADS_PER_VREG):
            h = grp * HEADS_PER_VREG + sub
            qh = jnp.where((lane // AT_DH) == sub, qp, 0.0) * scale
            sel_scr[h] = _topk_select_t(_dot3_nt(kmp, qh), own, nblk)
            qhb = (qh * LOG2E).astype(BF16)
            qh_scr[h] = qhb
            st = jnp.where(causal, _dot_nt(kb, qhb), NEG)
            s_scr[h, own] = st
            ms.append(jnp.max(st, axis=0, keepdims=True))

    def pass_a(n, ms):
        rows = pl.ds(pl.multiple_of(n * MOBA_BLOCK, MOBA_BLOCK), MOBA_BLOCK)
        out = []
        for grp in range(ngrp):
            kb = k_ref[rows, grp * LANES:(grp + 1) * LANES].astype(BF16)
            for sub in range(HEADS_PER_VREG):
                h = grp * HEADS_PER_VREG + sub
                selrow = jnp.max(jnp.where(rown == n, sel_scr[h], 0.0), axis=0, keepdims=True)
                st = jnp.where(selrow > 0.0, _dot_nt(kb, qh_scr[h]), NEG)
                s_scr[h, n] = st
                out.append(jnp.maximum(ms[h], jnp.max(st, axis=0, keepdims=True)))
        return tuple(out)

    def blocks(step, first, count, carry, unroll):
        while unroll >= 1:
            def body(i, c, first=first, unroll=unroll):
                for u in range(unroll):
                    c = step(first + unroll * i + u, c)
                return c
            trips = count // unroll
            carry = lax.fori_loop(0, trips, body, carry)
            first, count, unroll = first + trips * unroll, count - trips * unroll, unroll // 2
        return carry

    ms = blocks(pass_a, 0, own, tuple(ms), 4)

    acc_scr[...] = jnp.zeros_like(acc_scr)
    ones = jnp.ones((acc_scr.shape[1] - AT_DH, MOBA_BLOCK), BF16)

    def pass_b(n, carry):
        cols = pl.ds(pl.multiple_of(n * MOBA_BLOCK, MOBA_BLOCK), MOBA_BLOCK)
        for h in range(AT_HEADS):
            p = jnp.exp2(s_scr[h, n] - ms[h]).astype(BF16)
            vte = jnp.concatenate([vt_ref[0, h * AT_DH:(h + 1) * AT_DH, cols].astype(BF16), ones], axis=0)
            acc_scr[h] = acc_scr[h] + jnp.dot(vte, p, preferred_element_type=F32)
        return carry

    blocks(pass_b, 0, own + 1, 0, 4)
    for grp in range(ngrp):
        parts = []
        for h in range(grp * HEADS_PER_VREG, (grp + 1) * HEADS_PER_VREG):
            acc = acc_scr[h]
            parts.append(acc[:AT_DH] / acc[AT_DH:AT_DH + 1])
        ot = jnp.concatenate(parts, axis=0)
        o_ref[:, grp * LANES:(grp + 1) * LANES] = ot.T.astype(o_ref.dtype)


def _attn_prompt(q, k, vt, *, bsz, t):
    n, w = q.shape
    nblk = t // MOBA_BLOCK
    assert t % MOBA_BLOCK == 0
    rpad = -(-nblk // SUBLANES) * SUBLANES
    return pl.pallas_call(
        functools.partial(_attn_kernel, nblk=nblk),
        grid=(bsz, nblk),
        in_specs=[
            pl.BlockSpec((MOBA_BLOCK, w), lambda b, i: (b * nblk + i, 0)),
            pl.BlockSpec((t, w), lambda b, i: (b, 0)),
            pl.BlockSpec((1, w, t), lambda b, i: (b, 0, 0)),
        ],
        out_specs=pl.BlockSpec((MOBA_BLOCK, w), lambda b, i: (b * nblk + i, 0)),
        out_shape=jax.ShapeDtypeStruct((n, w), BF16),
        scratch_shapes=[pltpu.VMEM((rpad, w), F32),
                        pltpu.VMEM((AT_HEADS, MOBA_BLOCK, LANES), BF16),
                        pltpu.VMEM((AT_HEADS, rpad, MOBA_BLOCK), F32),
                        pltpu.VMEM((AT_HEADS, nblk, MOBA_BLOCK, MOBA_BLOCK), F32),
                        pltpu.VMEM((AT_HEADS, AT_DH + 2 * SUBLANES, MOBA_BLOCK), F32)],
        compiler_params=_cparams(("arbitrary", "arbitrary")),
        name="moba_prompt",
    )(q, k, vt)


def _samp_scores_kernel(*refs, npg, pages_per_blk):
    _scores_body(refs[1:5], refs[5:5 + npg], refs[5 + npg:], pl.program_id(1), pl.num_programs(1),
                 pages_per_blk=pages_per_blk)


def _scores_body(q_refs, kpages, out_refs, g, ng, *, pages_per_blk):
    qbd_ref, qm_ref, kn_ref, vn_ref = q_refs
    p_ref, oown_ref, pk_ref, s_scr, ks_scr = out_refs
    npg = len(kpages)
    nh, dh, ps = kpages[0].shape[2:]
    rows = qm_ref.shape[1]
    scale = dh ** -0.5
    lane = lax.broadcasted_iota(jnp.int32, (1, LANES), 1)

    @pl.when(g == 0)
    def _zero():
        ks_scr[...] = jnp.zeros_like(ks_scr)

    qb = qbd_ref[0] * scale
    qh = qb.astype(BF16)
    for i in range(npg // pages_per_blk):
        pages = [kpages[i * pages_per_blk + u][0, 0].reshape(nh * dh, ps) for u in range(pages_per_blk)]
        n = g * (npg // pages_per_blk) + i
        tot = pages[0]
        for pg in pages[1:]:
            tot = tot + pg
        ks_scr[...] = jnp.where(lane == n, jnp.sum(tot, axis=-1, keepdims=True), ks_scr[...])
        kp = pages[0] if pages_per_blk == 1 else jnp.concatenate(pages, axis=1)
        s_scr[:, pl.ds(pl.multiple_of(n * pages_per_blk * ps, ps), pages_per_blk * ps)] = jnp.dot(
            qh, kp.astype(BF16), preferred_element_type=F32)

    @pl.when(g == ng - 1)
    def _finish():
        total = s_scr.shape[1]
        bw = pages_per_blk * ps
        nblk = total // bw
        gate = jnp.where(lane < nblk, _dot3(qb, ks_scr[...] * (1.0 / bw)), NEG)
        lanef = lane.astype(F32)
        cur = gate
        picks = []
        for _ in range(min(MOBA_TOPK, nblk)):
            mx = jnp.max(cur, axis=-1, keepdims=True)
            idx = jnp.min(jnp.where(cur == mx, lanef, float(LANES)), axis=-1, keepdims=True)
            picks.append(idx)
            cur = jnp.where(lanef == idx, NEG, cur)
        pk = jnp.zeros((rows, LANES), F32)
        for j, idx in enumerate(picks):
            pk = jnp.where(lane == j, idx, pk)
        pk_ref[0] = pk.astype(jnp.int32)
        qm = qm_ref[0] * scale
        so = _dot_nt(qm, kn_ref[0])
        ro = lax.broadcasted_iota(jnp.int32, (rows, 1), 0)
        co = lax.broadcasted_iota(jnp.int32, (1, rows), 1)
        own_ok = jnp.logical_and(ro % nh == co % nh, co // nh <= ro // nh)
        so = jnp.where(own_ok, so, NEG)
        m = jnp.max(so, axis=-1, keepdims=True)
        lchunk = 8 * bw if total % (8 * bw) == 0 else bw
        nch = total // lchunk
        for cidx in range(nch):
            cs = slice(cidx * lchunk, (cidx + 1) * lchunk)
            bid = (cidx * (lchunk // bw) + lax.broadcasted_iota(jnp.int32, (1, lchunk), 1) // bw).astype(F32)
            hit = jnp.zeros((rows, lchunk), F32)
            for idx in picks:
                hit = jnp.where(bid == idx, 1.0, hit)
            sc = jnp.where(hit > 0.0, s_scr[:, cs], NEG)
            s_scr[:, cs] = sc
            m = jnp.maximum(m, jnp.max(sc, axis=-1, keepdims=True))
        po = jnp.exp(so - m)
        l = jnp.sum(po, axis=-1, keepdims=True)
        for cidx in range(nch):
            cs = slice(cidx * lchunk, (cidx + 1) * lchunk)
            pc = jnp.exp(s_scr[:, cs] - m)
            s_scr[:, cs] = pc
            l = l + jnp.sum(pc, axis=-1, keepdims=True)
        inv = 1.0 / l
        for cidx in range(nch):
            cs = slice(cidx * lchunk, (cidx + 1) * lchunk)
            p_ref[0, :, cs] = s_scr[:, cs] * inv
        oown_ref[0] = _dot(po * inv, vn_ref[0])


def _samp_pv_kernel(pt_ref, pk_ref, p_ref, oown_ref, cv_hbm, o_ref, vbuf, sem, *, layer, topk, pages_per_blk):
    b = pl.program_id(0)
    rows = p_ref.shape[1]
    nh, dh, ps = cv_hbm.shape[2:]

    def slice_copy(r, j, pg):
        page = pt_ref[b, pk_ref[b, r * topk + j] * pages_per_blk + pg]
        return pltpu.make_async_copy(cv_hbm.at[layer, page, r % nh], vbuf.at[r, j * pages_per_blk + pg], sem.at[r])

    slots = [(j, pg) for j in range(topk) for pg in range(pages_per_blk)]
    for r in range(rows):
        for i, (j, pg) in enumerate(slots):
            slice_copy(r, j, pg).start(priority=i % 2)

    lane = lax.broadcasted_iota(jnp.int32, (1, LANES), 1)
    ot = jnp.zeros((dh, LANES), F32)
    for r in range(rows):
        for j, pg in slots:
            pltpu.make_async_copy(cv_hbm.at[layer, 0, 0], vbuf.at[r, j * pages_per_blk + pg], sem.at[r]).wait()
        acc = jnp.zeros((dh, ps), F32)
        for j, pg in slots:
            off = pl.multiple_of((pk_ref[b, r * topk + j] * pages_per_blk + pg) * ps, ps)
            acc = acc + vbuf[r, j * pages_per_blk + pg] * p_ref[0, r:r + 1, pl.ds(off, ps)]
        ot = jnp.where(lane == r, jnp.sum(acc, axis=-1, keepdims=True), ot)
    o_ref[0] = oown_ref[0] + ot.T[:rows, :]


def _scores_plan(qm, kn, vn, cache_k, page_table, *, layer, seq_g):
    bs, rows, dh = qm.shape
    _, _, ps, nh, _ = cache_k.shape
    npages = page_table.shape[1]
    pages_per_blk = MOBA_BLOCK // ps
    assert MOBA_BLOCK % ps == 0 and npages % pages_per_blk == 0 and npages // pages_per_blk <= LANES
    assert rows <= LANES
    npg = math.gcd(npages, PAGES_PER_STEP)
    assert npg % pages_per_blk == 0
    total = npages * ps
    ck = jnp.transpose(cache_k, (0, 1, 3, 4, 2))
    head_of_row = jnp.arange(rows, dtype=jnp.int32) % nh
    head_of_col = jnp.arange(nh * dh, dtype=jnp.int32) // dh
    qbd = jnp.where(head_of_row[:, None] == head_of_col[None, :], jnp.tile(qm, (1, 1, nh)), 0.0)

    def per_seq(width):
        return pl.BlockSpec((1, rows, width), lambda *idx: (seq_g(*idx[:-1])[0], 0, 0))

    def page_spec(j):
        def index_map(*idx):
            seq, g = seq_g(*idx[:-1])
            return (layer, idx[-1][seq, g * npg + j], 0, 0, 0)
        return pl.BlockSpec((1, 1, nh, dh, ps), index_map)

    return dict(
        npg=npg, pages_per_blk=pages_per_blk, groups=npages // npg, bs=bs,
        in_specs=[per_seq(nh * dh), per_seq(dh), per_seq(dh), per_seq(dh)] + [page_spec(j) for j in range(npg)],
        args=[qbd, qm, kn, vn] + [ck] * npg,
        out_specs=[per_seq(total), per_seq(dh), per_seq(LANES)],
        out_shape=[jax.ShapeDtypeStruct((bs, rows, total), F32), jax.ShapeDtypeStruct((bs, rows, dh), F32),
                   jax.ShapeDtypeStruct((bs, rows, LANES), jnp.int32)],
        scratch=[pltpu.VMEM((rows, total), F32), pltpu.VMEM((nh * dh, LANES), F32)],
    )


def _samp_scores(plan, page_table):
    return pl.pallas_call(
        functools.partial(_samp_scores_kernel, npg=plan["npg"], pages_per_blk=plan["pages_per_blk"]),
        grid_spec=pltpu.PrefetchScalarGridSpec(
            num_scalar_prefetch=1, grid=(plan["bs"], plan["groups"]),
            in_specs=plan["in_specs"], out_specs=plan["out_specs"], scratch_shapes=plan["scratch"]),
        out_shape=plan["out_shape"],
        compiler_params=_cparams(("arbitrary", "arbitrary")),
        name="moba_sample_scores",
    )(page_table, *plan["args"])


def _hgrn_with_scores(hplan, splan, page_table):
    outs = pl.pallas_call(
        functools.partial(_hgrn_scores_kernel, n_hg_in=len(hplan["in_specs"]), npg=splan["npg"],
                          hgrn_cfg=hplan["cfg"], pages_per_blk=splan["pages_per_blk"]),
        grid_spec=pltpu.PrefetchScalarGridSpec(
            num_scalar_prefetch=1, grid=hplan["grid"],
            in_specs=hplan["in_specs"] + splan["in_specs"],
            out_specs=hplan["out_specs"] + splan["out_specs"],
            scratch_shapes=hplan["scratch"] + splan["scratch"]),
        out_shape=hplan["out_shape"] + splan["out_shape"],
        compiler_params=_cparams(("arbitrary", "arbitrary", "arbitrary")),
        name="hgrn2_with_sample_scores",
    )(page_table, *hplan["args"], *splan["args"])
    return outs[:2], outs[2:]


def _samp_pv(probs, o_own, picks, cache_v, page_table, *, layer):
    bs, rows, total = probs.shape
    _, _, ps, nh, dh = cache_v.shape
    pages_per_blk = MOBA_BLOCK // ps
    topk = min(MOBA_TOPK, page_table.shape[1] // pages_per_blk)
    cv = jnp.transpose(cache_v, (0, 1, 3, 4, 2))
    picks = picks[:, :, :topk].reshape(bs, rows * topk)
    seq_spec = lambda w: pl.BlockSpec((1, rows, w), lambda b, pt, pk: (b, 0, 0))
    return pl.pallas_call(
        functools.partial(_samp_pv_kernel, layer=layer, topk=topk, pages_per_blk=pages_per_blk),
        grid_spec=pltpu.PrefetchScalarGridSpec(
            num_scalar_prefetch=2,
            grid=(bs,),
            in_specs=[seq_spec(total), seq_spec(dh), pl.BlockSpec(memory_space=pl.ANY)],
            out_specs=seq_spec(dh),
            scratch_shapes=[pltpu.VMEM((rows, topk * pages_per_blk, dh, ps), F32), pltpu.SemaphoreType.DMA((rows,))],
        ),
        out_shape=jax.ShapeDtypeStruct((bs, rows, dh), F32),
        compiler_params=_cparams(("arbitrary",)),
        name="moba_sample_pv",
    )(page_table, picks, probs, o_own, cv)


def _mix_ffn_kernel(x_ref, oa_ref, ob_ref, gate_ref, gt1_ref, sc2_ref, sh2_ref, gt2_ref,
                    gpost1_ref, gpre2_ref, gpost2_ref, wa_ref, wb_ref, wo_ref, wgu_ref, wd_ref, out_ref,
                    *, dff, fchunk):
    d = x_ref.shape[1]
    merged = gate_ref[:, :d] * _dot(oa_ref[...], wa_ref[...]) + gate_ref[:, d:] * _dot(ob_ref[...], wb_ref[...])
    x1 = x_ref[...] + gt1_ref[0] * _rms(_dot(merged, wo_ref[...]), gpost1_ref[...])
    hb = (_rms(x1, gpre2_ref[...]) * (1.0 + sc2_ref[0]) + sh2_ref[0]).astype(BF16)
    acc = jnp.zeros(x1.shape, F32)
    for c0 in range(0, dff, fchunk):
        c1 = min(c0 + fchunk, dff)
        g = _dot(hb, wgu_ref[:, c0:c1])
        u = _dot(hb, wgu_ref[:, dff + c0:dff + c1])
        acc = acc + _dot(g * _sigmoid(g) * u, wd_ref[c0:c1, :])
    out_ref[...] = x1 + gt2_ref[0] * _rms(acc, gpost2_ref[...])


def _mix_ffn(x, oa, ob, gates, gt1, sc2, sh2, gt2, g_post1, g_pre2, g_post2, wa, wb, wo, wgu, wd, tm):
    n, d = x.shape
    dff = wd.shape[0]
    nt = n // tm
    nbm, r, _ = gt1.shape
    tps = nt // nbm
    fchunk = 3 * MXU_WIDTH
    resident = dict(pipeline_mode=pl.Buffered(1))
    rowspec = lambda w: pl.BlockSpec((tm, w), lambda i: (i, 0))
    modspec = pl.BlockSpec((1, r, d), lambda i: (i // tps, 0, 0))
    vecspec = pl.BlockSpec((1, d), lambda i: (0, 0))
    wspec = lambda w: pl.BlockSpec(w.shape, lambda i: (0, 0), **resident)
    return pl.pallas_call(
        functools.partial(_mix_ffn_kernel, dff=dff, fchunk=fchunk),
        grid=(nt,),
        in_specs=[rowspec(d), rowspec(oa.shape[1]), rowspec(ob.shape[1]), rowspec(gates.shape[1]),
                  modspec, modspec, modspec, modspec, vecspec, vecspec, vecspec,
                  wspec(wa), wspec(wb), wspec(wo), wspec(wgu), wspec(wd)],
        out_specs=rowspec(d),
        out_shape=jax.ShapeDtypeStruct((n, d), F32),
        compiler_params=_cparams(("arbitrary",)),
        name="merge_proj_ffn",
    )(x, oa, ob, gates, gt1, sc2, sh2, gt2, g_post1.reshape(1, d), g_pre2.reshape(1, d), g_post2.reshape(1, d),
      wa, wb, wo, wgu, wd)


def _rope_tables(pos):
    half = AT_DH // 2
    inv = ROPE_THETA ** (-jnp.arange(half, dtype=F32) / half)
    ang = pos.astype(F32)[:, None] * inv[None, :]
    cos = jnp.cos(ang)
    sin = jnp.sin(ang)
    cos_h = jnp.concatenate([cos, cos], axis=-1)
    sin_h = jnp.concatenate([-sin, sin], axis=-1)
    return jnp.tile(cos_h, (1, HEADS_PER_VREG)), jnp.tile(sin_h, (1, HEADS_PER_VREG))


def _project(x2d, mods, *, t, pos0, weights, tm, transposed_kv):
    sh1, sc1 = mods[0], mods[1]
    g_pre1, w_in_hi, wqk_lo = weights[1], weights[3], weights[4]
    cos_t, sin_t = _rope_tables(pos0 + jnp.arange(t, dtype=jnp.int32))
    if tm > t:
        cos_t = jnp.tile(cos_t, (tm // t, 1))
        sin_t = jnp.tile(sin_t, (tm // t, 1))
    return _in_proj(x2d, sc1, sh1, g_pre1, cos_t, sin_t, w_in_hi, wqk_lo, tm, seq_len=t if transposed_kv else None)


def _layer(xp, xs, mods_p, mods_s, s0_s, cache_k, cache_v, page_table, *, layer, weights, dims):
    bp, t, bs, ts, past_len = dims
    (hg_lb, g_pre1, g_post1, w_in_hi, wqk_lo, g_onorm, wa, wb, wo, g_pre2, g_post2, wgu, wd) = weights
    tm_p, tm_s = 512, bs * ts
    hg4_p, q_p, k_p, kt_p, vt_p, gates_p = _project(xp, mods_p, t=t, pos0=0, weights=weights, tm=tm_p,
                                                     transposed_kv=True)
    hg4_s, q_s, k_s, v_s, gates_s = _project(xs, mods_s, t=ts, pos0=past_len, weights=weights, tm=tm_s,
                                             transposed_kv=False)

    chunk = HG_CHUNK if t % HG_CHUNK == 0 else t
    hplan = _hgrn_plan(hg4_p, hg_lb, g_onorm, None, layer=layer, bsz=bp, t=t, c=chunk, bd=SUBLANES, valid=chunk,
                       rows_per_step=512, hps=1, out_dtype=BF16)
    to_rows = lambda a: a.reshape(bs, ts * AT_HEADS, AT_DH)
    groups = page_table.shape[1] // math.gcd(page_table.shape[1], PAGES_PER_STEP)
    _, hgroups, steps = hplan["grid"]
    fused = bp * hgroups * steps == bs * groups
    seq_g = ((lambda b, hg, cb: divmod((b * hgroups + hg) * steps + cb, groups)) if fused
             else (lambda b, g: (b, g)))
    splan = _scores_plan(to_rows(q_s), to_rows(k_s), to_rows(v_s), cache_k, page_table, layer=layer, seq_g=seq_g)
    if fused:
        (oa_p, state_p), (probs, o_own, picks) = _hgrn_with_scores(hplan, splan, page_table)
    else:
        oa_p, state_p = _hgrn(hg4_p, hg_lb, g_onorm, None, layer=layer, bsz=bp, t=t, c=chunk, bd=SUBLANES,
                              valid=chunk, rows_per_step=512, hps=1, out_dtype=BF16)
        probs, o_own, picks = _samp_scores(splan, page_table)
    ob_s = _samp_pv(probs, o_own, picks, cache_v, page_table, layer=layer).reshape(bs * ts, AT_HEADS * AT_DH)

    cs = -(-ts // SUBLANES) * SUBLANES
    hg4_sp = jnp.pad(hg4_s.reshape(bs, ts, -1), ((0, 0), (0, cs - ts), (0, 0))).reshape(bs * cs, -1)
    oa_s, state_s = _hgrn(hg4_sp, hg_lb, g_onorm, s0_s, layer=layer, bsz=bs, t=cs, c=cs, bd=cs, valid=ts,
                          rows_per_step=cs, hps=HG_HEADS, out_dtype=F32)
    oa_s = oa_s.reshape(bs, cs, -1)[:, :ts].reshape(bs * ts, -1)

    ob_p = _attn_prompt(q_p, k_p, vt_p, bsz=bp, t=t)
    from_t = lambda a: jnp.transpose(a.reshape(bp, AT_HEADS, AT_DH, t), (0, 3, 1, 2))
    tail = (g_post1, g_pre2, g_post2, wa, wb, wo, wgu, wd)
    xp = _mix_ffn(xp, oa_p, ob_p, gates_p, mods_p[2], mods_p[4], mods_p[3], mods_p[5], *tail, tm_p)
    xs = _mix_ffn(xs, oa_s, ob_s, gates_s, mods_s[2], mods_s[4], mods_s[3], mods_s[5], *tail, tm_s)
    return (xp, xs, state_p, state_s, from_t(kt_p), from_t(vt_p),
            k_s.reshape(bs, ts, AT_HEADS, AT_DH), v_s.reshape(bs, ts, AT_HEADS, AT_DH))


def kernel(x_prompt, x_sample, c_prompt, c_sample, state_hgrn, cache_k, cache_v, page_table, hg_lb, w_ada, b_ada,
           g_pre1, g_post1, w_in, g_onorm, w_proj_a, w_proj_b, w_out, g_pre2, g_post2, w_gu, w_down):
    bp, t, d = x_prompt.shape
    bs, ts, _ = x_sample.shape
    depth = w_in.shape[0]
    past_len = page_table.shape[1] * cache_k.shape[2]
    assert past_len % MOBA_BLOCK == 0 and ts <= MOBA_BLOCK
    hgw = 2 * HG_HEADS * HG_DK + 2 * HG_HEADS * HG_DV
    atw = AT_HEADS * AT_DH

    xp = x_prompt.reshape(bp * t, d)
    xs = x_sample.reshape(bs * ts, d)
    c_all = jnp.concatenate([c_prompt, c_sample], axis=0)
    outs = [[] for _ in range(6)]
    for l in range(depth):
        mod = _ada_mod(c_all, w_ada[l], b_ada[l])
        mods_p = [m[:bp].reshape(bp, 1, d) for m in jnp.split(mod, 6, axis=-1)]
        mods_s = [jnp.repeat(m[bp:], ts, axis=0).reshape(1, bs * ts, d) for m in jnp.split(mod, 6, axis=-1)]
        _, wqk_lo = _split_weights(w_in[l][:, hgw:hgw + 2 * atw])
        weights = (hg_lb, g_pre1[l], g_post1[l], w_in[l].astype(BF16), wqk_lo, g_onorm[l],
                   w_proj_a[l].astype(BF16), w_proj_b[l].astype(BF16), w_out[l].astype(BF16),
                   g_pre2[l], g_post2[l], w_gu[l].astype(BF16), w_down[l].astype(BF16))
        xp, xs, *rest = _layer(xp, xs, mods_p, mods_s, state_hgrn[l], cache_k, cache_v, page_table, layer=l,
                               weights=weights, dims=(bp, t, bs, ts, past_len))
        for lst, val in zip(outs, rest):
            lst.append(val)
    return (xp.reshape(bp, t, d), xs.reshape(bs, ts, d)) + tuple(jnp.stack(o) for o in outs)
```
